```python
import math
import jax, jax.numpy as jnp
from jax import lax
import numpy as np

D_MODEL = 1024
BATCH = 2
SEQ = 16384
DEPTH = 2

N_MIXERS = 2
EPS = 1e-6

CONV_WIDTH = 2 * D_MODEL
CONV_K = 3

HEAD_DIM = D_MODEL // 16
DIL_PAIRS = ((128, 1), (512, 4), (2048, 16))
N_GROUPS = len(DIL_PAIRS)
ATTN_WIDTH = (3 * D_MODEL) // 2
HEADS_PER_GROUP = ATTN_WIDTH // (N_GROUPS * HEAD_DIM)
N_HEADS = N_GROUPS * HEADS_PER_GROUP
Q_BLOCK = 128

REL_BUCKETS = 32
REL_MAX_DIST = 1024

kernel_name = "hybrid_shortconv_dilated_attn_encoder"


def rmsnorm(x, g):
    xf = x.astype(jnp.float32)
    y = xf * lax.rsqrt(jnp.mean(xf * xf, axis=-1, keepdims=True) + EPS)
    return (y * g.astype(jnp.float32)).astype(x.dtype)


def t5_bucket(rel):
    nb = REL_BUCKETS // 2
    ret = (rel > 0).astype(np.int32) * nb
    n = np.abs(rel)
    max_exact = nb // 2
    large = max_exact + (np.log(np.maximum(n, 1) / max_exact)
                         / np.log(REL_MAX_DIST / max_exact) * (nb - max_exact)).astype(np.int32)
    large = np.minimum(large, nb - 1)
    return ret + np.where(n < max_exact, n, large).astype(np.int32)


def short_conv_mixer(hn, w_in, conv_w, conv_b, w_out):
    proj = hn @ w_in
    b_gate, c_gate, u, z = jnp.split(proj, 4, axis=-1)
    cu = lax.conv_general_dilated(
        c_gate * u, conv_w[:, None, :].astype(u.dtype),
        window_strides=(1,), padding=((1, 1),),
        dimension_numbers=('NWC', 'WIO', 'NWC'),
        feature_group_count=CONV_WIDTH) + conv_b
    y = b_gate * cu * jax.nn.silu(z)
    return y @ w_out


def dilated_attention_mixer(hn, w_in, q_gain, k_gain, rel_table, w_out):
    bsz, s, _ = hn.shape
    proj = hn @ w_in
    qkv = proj[..., :3 * ATTN_WIDTH].reshape(bsz, s, N_GROUPS, 3, HEADS_PER_GROUP, HEAD_DIM)
    z = proj[..., 3 * ATTN_WIDTH:]
    gshape = (N_GROUPS, HEADS_PER_GROUP, HEAD_DIM)
    q = rmsnorm(qkv[:, :, :, 0], q_gain.reshape(gshape)) * (HEAD_DIM ** -0.5)
    k = rmsnorm(qkv[:, :, :, 1], k_gain.reshape(gshape))
    v = qkv[:, :, :, 2]
    n_blk = s // Q_BLOCK
    starts = jnp.arange(n_blk, dtype=jnp.int32) * Q_BLOCK

    outs, lses = [], []
    for g, (window, dil) in enumerate(DIL_PAIRS):
        half = (window // 2) // dil
        offs_np = (np.arange(-half, half + 1) * dil).astype(np.int32)
        bias = rel_table[t5_bucket(offs_np)][:, g * HEADS_PER_GROUP:(g + 1) * HEADS_PER_GROUP]
        bias = bias.astype(jnp.float32).T
        offs = jnp.asarray(offs_np)
        qg, kg, vg = q[:, :, g], k[:, :, g], v[:, :, g]

        def block(start, qg=qg, kg=kg, vg=vg, offs=offs, bias=bias):
            qb = lax.dynamic_slice_in_dim(qg, start, Q_BLOCK, axis=1)
            kpos = start + jnp.arange(Q_BLOCK, dtype=jnp.int32)[:, None] + offs[None, :]
            valid = (kpos >= 0) & (kpos < s)
            kidx = jnp.clip(kpos, 0, s - 1)
            kb = jnp.take(kg, kidx, axis=1)
            vb = jnp.take(vg, kidx, axis=1)
            logits = jnp.einsum('bqhd,bqkhd->bqhk', qb, kb).astype(jnp.float32) + bias[None, None]
            logits = jnp.where(valid[None, :, None, :], logits, -jnp.inf)
            lse = jax.nn.logsumexp(logits, axis=-1)
            p = jnp.exp(logits - lse[..., None])
            o = jnp.einsum('bqhk,bqkhd->bqhd', p.astype(vb.dtype), vb)
            return o, lse

        o, lse = lax.map(block, starts)
        outs.append(jnp.moveaxis(o, 0, 1).reshape(bsz, s, HEADS_PER_GROUP, HEAD_DIM))
        lses.append(jnp.moveaxis(lse, 0, 1).reshape(bsz, s, HEADS_PER_GROUP))

    o_all = jnp.stack(outs, axis=2)
    alpha = jax.nn.softmax(jnp.stack(lses, axis=2), axis=2)
    y = (o_all * alpha[..., None].astype(o_all.dtype)).reshape(bsz, s, ATTN_WIDTH)
    y = y * jax.nn.silu(z)
    return y @ w_out


def setup_inputs(seed: int = 0) -> dict:
    key = jax.random.key(seed)
    ks = jax.random.split(key, 12)
    n_a = (DEPTH + 1) // 2
    n_b = DEPTH // 2
    f32 = jnp.float32
    x = jax.random.normal(ks[0], (BATCH, SEQ, D_MODEL), f32)
    norm_g = 1.0 + 0.02 * jax.random.normal(ks[1], (DEPTH, D_MODEL), f32)
    conv_w_in = jax.random.normal(ks[2], (n_a, D_MODEL, 4 * CONV_WIDTH), f32) * D_MODEL ** -0.5
    conv_kernel = jax.random.normal(ks[3], (n_a, CONV_K, CONV_WIDTH), f32) * CONV_K ** -0.5
    conv_bias = 0.02 * jax.random.normal(ks[4], (n_a, CONV_WIDTH), f32)
    conv_w_out = jax.random.normal(ks[5], (n_a, CONV_WIDTH, D_MODEL), f32) * CONV_WIDTH ** -0.5
    attn_w_in = jax.random.normal(ks[6], (n_b, D_MODEL, 4 * ATTN_WIDTH), f32) * D_MODEL ** -0.5
    q_norm_g = 1.0 + 0.02 * jax.random.normal(ks[7], (n_b, N_HEADS, HEAD_DIM), f32)
    k_norm_g = 1.0 + 0.02 * jax.random.normal(ks[8], (n_b, N_HEADS, HEAD_DIM), f32)
    attn_w_out = jax.random.normal(ks[9], (n_b, ATTN_WIDTH, D_MODEL), f32) * ATTN_WIDTH ** -0.5
    rel_bias_table = 0.5 * jax.random.normal(ks[10], (REL_BUCKETS, N_HEADS), f32)
    return {"x": x, "norm_g": norm_g, "conv_w_in": conv_w_in, "conv_kernel": conv_kernel,
            "conv_bias": conv_bias, "conv_w_out": conv_w_out, "attn_w_in": attn_w_in,
            "q_norm_g": q_norm_g, "k_norm_g": k_norm_g, "attn_w_out": attn_w_out,
            "rel_bias_table": rel_bias_table}


def reference(x, norm_g, conv_w_in, conv_kernel, conv_bias, conv_w_out, attn_w_in,
              q_norm_g, k_norm_g, attn_w_out, rel_bias_table):
    h = x
    for i in range(DEPTH):
        hn = rmsnorm(h, norm_g[i])
        j = i // N_MIXERS
        if i % N_MIXERS == 0:
            h = h + short_conv_mixer(hn, conv_w_in[j], conv_kernel[j], conv_bias[j], conv_w_out[j])
        else:
            h = h + dilated_attention_mixer(hn, attn_w_in[j], q_norm_g[j], k_norm_g[j],
                                            rel_bias_table, attn_w_out[j])
    return h
```

```python
import functools

import jax
import jax.numpy as jnp
import numpy as np
from jax import lax
from jax.experimental import pallas as pl
from jax.experimental.pallas import tpu as pltpu

EPS = 1e-6
HEAD_DIM = 64
HEADS_PER_GROUP = 8
GROUP_WIDTH = HEAD_DIM * HEADS_PER_GROUP
DIL_PAIRS = ((128, 1), (512, 4), (2048, 16))
N_GROUPS = len(DIL_PAIRS)
HALF_WINDOW = 64
REL_BUCKETS = 32
REL_MAX_DIST = 1024
MASKED = -1e30

LANES = 128
Q_BLOCK = 128
K_WINDOW = Q_BLOCK + 2 * HALF_WINDOW
VMEM_LIMIT_BYTES = 56 * 1024 * 1024

BF16 = jnp.bfloat16
F32 = jnp.float32


def _silu(z):
    return z / (1.0 + jnp.exp(-z))


def _rmsnorm(x, g):
    ms = jnp.mean(x * x, axis=-1, keepdims=True)
    return x * lax.rsqrt(ms + EPS) * g


def _resident(shape):
    zeros = (0,) * len(shape)
    return pl.BlockSpec(shape, lambda *_: zeros, pipeline_mode=pl.Buffered(1))


CONV_TOKENS = 512
CONV_HALO = 16
CONV_CHUNK = 512


def _conv_layer_kernel(xp_ref, x_ref, xn_ref, g_ref, wcu_ref, wbz_ref, cw_ref, cb_ref, wo_ref,
                       o_ref, *, n_chunks):
    i = pl.program_id(1)
    last = pl.num_programs(1) - 1
    tm, halo, ce = CONV_TOKENS, CONV_HALO, CONV_CHUNK
    g = g_ref[...]
    x = x_ref[0]
    xp = jnp.where(i > 0, xp_ref[0], 0.0)
    xn = jnp.where(i < last, xn_ref[0], 0.0)
    hn = _rmsnorm(x, g).astype(BF16)
    hne = jnp.concatenate(
        [_rmsnorm(xp, g).astype(BF16), hn, _rmsnorm(xn, g).astype(BF16)], axis=0)
    o_ref[0] = x

    def chunk(j, carry):
        cu = jnp.dot(hne, wcu_ref[j], preferred_element_type=F32)
        bz = jnp.dot(hn, wbz_ref[j], preferred_element_type=F32)
        p = cu[:, :ce] * cu[:, ce:]
        cw = cw_ref[j]
        conv = (cw[0:1] * p[halo - 1:halo - 1 + tm] + cw[1:2] * p[halo:halo + tm]
                + cw[2:3] * p[halo + 1:halo + 1 + tm] + cb_ref[j])
        y = bz[:, :ce] * conv * _silu(bz[:, ce:])
        o_ref[0] += jnp.dot(y.astype(BF16), wo_ref[j], preferred_element_type=F32)
        return carry

    lax.fori_loop(0, n_chunks, chunk, 0)


def _conv_layer(x, g, w_in, conv_w, conv_b, w_out):
    bsz, s, d = x.shape
    e = w_out.shape[0]
    tm, halo, ce = CONV_TOKENS, CONV_HALO, CONV_CHUNK
    assert s % tm == 0 and tm % halo == 0 and e % ce == 0 and w_in.shape == (d, 4 * e)
    n_chunks = e // ce
    w4 = w_in.astype(BF16).reshape(d, 4, n_chunks, ce)
    wcu = jnp.transpose(w4[:, 1:3], (2, 0, 1, 3)).reshape(n_chunks, d, 2 * ce)
    wbz = jnp.transpose(w4[:, 0::3], (2, 0, 1, 3)).reshape(n_chunks, d, 2 * ce)
    cw = jnp.transpose(conv_w.reshape(3, n_chunks, ce), (1, 0, 2))
    cb = conv_b.reshape(n_chunks, 1, ce)
    wo = w_out.astype(BF16).reshape(n_chunks, ce, d)
    per_tile = tm // halo
    n_halo_blocks = s // halo
    return pl.pallas_call(
        functools.partial(_conv_layer_kernel, n_chunks=n_chunks),
        name="conv_layer",
        grid=(bsz, s // tm),
        in_specs=[
            pl.BlockSpec((1, halo, d), lambda b, i: (b, jnp.maximum(i * per_tile - 1, 0), 0)),
            pl.BlockSpec((1, tm, d), lambda b, i: (b, i, 0)),
            pl.BlockSpec((1, halo, d),
                         lambda b, i: (b, jnp.minimum((i + 1) * per_tile, n_halo_blocks - 1), 0)),
            _resident((1, d)),
            _resident((n_chunks, d, 2 * ce)),
            _resident((n_chunks, d, 2 * ce)),
            _resident((n_chunks, 3, ce)),
            _resident((n_chunks, 1, ce)),
            _resident((n_chunks, ce, d)),
        ],
        out_specs=pl.BlockSpec((1, tm, d), lambda b, i: (b, i, 0)),
        out_shape=jax.ShapeDtypeStruct(x.shape, x.dtype),
        compiler_params=pltpu.CompilerParams(
            dimension_semantics=("arbitrary", "arbitrary"),
            vmem_limit_bytes=VMEM_LIMIT_BYTES),
    )(x, x, x, g.reshape(1, d), wcu, wbz, cw, cb, wo)


PROJ_TOKENS = 512


def _attn_proj_kernel(h_ref, g_ref, wqkv_ref, wz_ref, gain_ref, headmean_ref, qkv_ref, z_ref):
    gw = GROUP_WIDTH
    hn = _rmsnorm(h_ref[0], g_ref[...]).astype(BF16)
    for c in range(3 * N_GROUPS):
        y = jnp.dot(hn, wqkv_ref[c], preferred_element_type=F32)
        if c % 3 < 2:
            ms = jnp.dot((y * y).astype(BF16), headmean_ref[...], preferred_element_type=F32)
            y = y * lax.rsqrt(ms + EPS) * gain_ref[c]
        qkv_ref[c, 0] = y.astype(BF16)
    for c in range(N_GROUPS):
        z = jnp.dot(hn, wz_ref[:, c * gw:(c + 1) * gw], preferred_element_type=F32)
        z_ref[0, :, c * gw:(c + 1) * gw] = z.astype(BF16)


def _attn_proj(h, g, w_in, q_gain, k_gain):
    bsz, s, d = h.shape
    gw, tm = GROUP_WIDTH, PROJ_TOKENS
    n_qkv = 3 * N_GROUPS
    assert s % tm == 0 and w_in.shape == (d, (n_qkv + N_GROUPS) * gw)
    wb = w_in.astype(BF16)
    wqkv = jnp.transpose(wb[:, :n_qkv * gw].reshape(d, n_qkv, gw), (1, 0, 2))
    wz = wb[:, n_qkv * gw:]
    gains = jnp.stack([q_gain.reshape(N_GROUPS, gw) * HEAD_DIM ** -0.5,
                       k_gain.reshape(N_GROUPS, gw),
                       jnp.ones((N_GROUPS, gw), F32)], axis=1).reshape(n_qkv, 1, gw)
    head = np.arange(gw) // HEAD_DIM
    headmean = jnp.asarray((head[:, None] == head[None, :]) / HEAD_DIM, BF16)
    return pl.pallas_call(
        _attn_proj_kernel,
        name="attn_proj",
        grid=(bsz, s // tm),
        in_specs=[
            pl.BlockSpec((1, tm, d), lambda b, i: (b, i, 0)),
            _resident((1, d)),
            _resident((n_qkv, d, gw)),
            _resident((d, N_GROUPS * gw)),
            _resident((n_qkv, 1, gw)),
            _resident((gw, gw)),
        ],
        out_specs=[
            pl.BlockSpec((n_qkv, 1, tm, gw), lambda b, i: (0, b, i, 0)),
            pl.BlockSpec((1, tm, N_GROUPS * gw), lambda b, i: (b, i, 0)),
        ],
        out_shape=[
            jax.ShapeDtypeStruct((n_qkv, bsz, s, gw), BF16),
            jax.ShapeDtypeStruct((bsz, s, N_GROUPS * gw), BF16),
        ],
        compiler_params=pltpu.CompilerParams(
            dimension_semantics=("arbitrary", "arbitrary"),
            vmem_limit_bytes=VMEM_LIMIT_BYTES),
    )(h, g.reshape(1, d), wqkv, wz, gains, headmean)


def _band_attn_kernel(q_ref, kp_ref, kc_ref, kn_ref, vp_ref, vc_ref, vn_ref, bias_ref,
                      headmask_ref, o_ref, lse_ref, kwin_ref, vwin_ref):
    hw, qb = HALF_WINDOW, Q_BLOCK
    for win_ref, prev_ref, cur_ref, next_ref in ((kwin_ref, kp_ref, kc_ref, kn_ref),
                                                 (vwin_ref, vp_ref, vc_ref, vn_ref)):
        win_ref[0:hw] = prev_ref[0]
        win_ref[hw:hw + qb] = cur_ref[0]
        win_ref[hw + qb:] = next_ref[0]
    lane = lax.broadcasted_iota(jnp.int32, (qb, LANES), 1)
    low_half = lane < HEAD_DIM
    lse_tile = jnp.zeros((qb, LANES), F32)
    for pair in range(HEADS_PER_GROUP // 2):
        cols = slice(pair * LANES, (pair + 1) * LANES)
        q2 = q_ref[0, :, cols]
        qq = jnp.concatenate([q2 * headmask_ref[0], q2 * headmask_ref[1]], axis=0)
        s = lax.dot_general(qq, kwin_ref[:, cols], (((1,), (1,)), ((), ())),
                            preferred_element_type=F32)
        s = s + bias_ref[0, pair]
        m = jnp.max(s, axis=-1, keepdims=True)
        e = jnp.exp(s - m)
        l = jnp.sum(e, axis=-1, keepdims=True)
        o2 = jnp.dot(e.astype(BF16), vwin_ref[:, cols], preferred_element_type=F32) / l
        o_ref[0, :, cols] = jnp.where(low_half, o2[:qb], o2[qb:]).astype(BF16)
        lse2 = m + jnp.log(l)
        lse_tile = jnp.where(lane == 2 * pair, lse2[:qb], lse_tile)
        lse_tile = jnp.where(lane == 2 * pair + 1, lse2[qb:], lse_tile)
    lse_ref[0] = lse_tile


def _t5_bucket(rel):
    nb = REL_BUCKETS // 2
    ret = (rel > 0).astype(np.int32) * nb
    n = np.abs(rel)
    max_exact = nb // 2
    large = max_exact + (np.log(np.maximum(n, 1) / max_exact)
                         / np.log(REL_MAX_DIST / max_exact) * (nb - max_exact)).astype(np.int32)
    large = np.minimum(large, nb - 1)
    return ret + np.where(n < max_exact, n, large).astype(np.int32)


def _band_bias(rel_table, group, dil):
    hw, qb, kw = HALF_WINDOW, Q_BLOCK, K_WINDOW
    buckets = _t5_bucket(np.arange(-hw, hw + 1) * dil)
    heads = slice(group * HEADS_PER_GROUP, (group + 1) * HEADS_PER_GROUP)
    per_offset = rel_table[buckets][:, heads].astype(F32).T
    kj = np.arange(kw)[None, :]
    rel = kj - np.arange(qb)[:, None]
    in_band = (rel >= 0) & (rel <= 2 * hw)
    mid = jnp.where(in_band[None], per_offset[:, np.clip(rel, 0, 2 * hw)], MASKED)
    first = jnp.where((kj >= hw)[None], mid, MASKED)
    final = jnp.where((kj < hw + qb)[None], mid, MASKED)
    tiles = jnp.stack([first, mid, final])
    return tiles.reshape(3, HEADS_PER_GROUP // 2, 2 * qb, kw)


def _band_attention(q, k, v, bias):
    n_seq, length, gw = q.shape
    hw, qb, kw = HALF_WINDOW, Q_BLOCK, K_WINDOW
    n_blocks = length // qb
    assert length % qb == 0 and n_blocks >= 2 and qb == 2 * hw
    headmask = jnp.asarray(
        (np.arange(LANES)[None, :] // HEAD_DIM == np.arange(2)[:, None])[:, None, :], BF16)
    cur = pl.BlockSpec((1, qb, gw), lambda s, i: (s, i, 0))
    prev = pl.BlockSpec((1, hw, gw), lambda s, i: (s, jnp.maximum(2 * i - 1, 0), 0))
    nxt = pl.BlockSpec((1, hw, gw), lambda s, i: (s, jnp.minimum(2 * i + 2, 2 * n_blocks - 1), 0))
    variant = lambda s, i: ((i > 0).astype(jnp.int32) + (i == n_blocks - 1).astype(jnp.int32),
                            0, 0, 0)
    return pl.pallas_call(
        _band_attn_kernel,
        name="band_attn",
        grid=(n_seq, n_blocks),
        in_specs=[cur, prev, cur, nxt, prev, cur, nxt,
                  pl.BlockSpec((1,) + bias.shape[1:], variant),
                  _resident((2, 1, LANES))],
        out_specs=[cur, pl.BlockSpec((1, qb, LANES), lambda s, i: (s, i, 0))],
        out_shape=[jax.ShapeDtypeStruct(q.shape, BF16),
                   jax.ShapeDtypeStruct((n_seq, length, LANES), F32)],
        scratch_shapes=[pltpu.VMEM((kw, gw), BF16), pltpu.VMEM((kw, gw), BF16)],
        compiler_params=pltpu.CompilerParams(
            dimension_semantics=("arbitrary", "arbitrary"),
            vmem_limit_bytes=VMEM_LIMIT_BYTES),
    )(q, k, k, k, v, v, v, bias, headmask)


OUT_TOKENS = 512


def _attn_out_kernel(h_ref, o0_ref, o1_ref, o2_ref, l0_ref, l1_ref, l2_ref, z_ref, expand_ref,
                     wo_ref, out_ref):
    gw = GROUP_WIDTH
    lses = [l0_ref[0], l1_ref[0], l2_ref[0]]
    top = jnp.maximum(jnp.maximum(lses[0], lses[1]), lses[2])
    es = [jnp.exp(l - top) for l in lses]
    denom = es[0] + es[1] + es[2]
    acc = h_ref[0]
    for g, o_ref in enumerate((o0_ref, o1_ref, o2_ref)):
        alpha = es[g] / denom
        hi = alpha.astype(BF16)
        lo = (alpha - hi.astype(F32)).astype(BF16)
        spread = (jnp.dot(hi, expand_ref[...], preferred_element_type=F32)
                  + jnp.dot(lo, expand_ref[...], preferred_element_type=F32))
        z = z_ref[0, :, g * gw:(g + 1) * gw].astype(F32)
        y = o_ref[0].astype(F32) * spread * _silu(z)
        acc = acc + jnp.dot(y.astype(BF16), wo_ref[g], preferred_element_type=F32)
    out_ref[0] = acc


def _attn_out(h, outs, lses, z, w_out):
    bsz, s, d = h.shape
    gw, tm = GROUP_WIDTH, OUT_TOKENS
    assert s % tm == 0 and w_out.shape == (N_GROUPS * gw, d)
    expand = jnp.asarray(np.arange(LANES)[:, None] == np.arange(gw)[None, :] // HEAD_DIM, BF16)
    tok = lambda width: pl.BlockSpec((1, tm, width), lambda b, i: (b, i, 0))
    return pl.pallas_call(
        _attn_out_kernel,
        name="attn_out",
        grid=(bsz, s // tm),
        in_specs=[tok(d), tok(gw), tok(gw), tok(gw), tok(LANES), tok(LANES), tok(LANES),
                  tok(N_GROUPS * gw), _resident((LANES, gw)), _resident((N_GROUPS, gw, d))],
        out_specs=tok(d),
        out_shape=jax.ShapeDtypeStruct(h.shape, h.dtype),
        compiler_params=pltpu.CompilerParams(
            dimension_semantics=("arbitrary", "arbitrary"),
            vmem_limit_bytes=VMEM_LIMIT_BYTES),
    )(h, *outs, *lses, z, expand, w_out.astype(BF16).reshape(N_GROUPS, gw, d))


def _to_classes(a, dil):
    bsz, s, w = a.shape
    if dil == 1:
        return a
    return jnp.transpose(a.reshape(bsz, s // dil, dil, w), (0, 2, 1, 3)).reshape(
        bsz * dil, s // dil, w)


def _from_classes(a, dil, bsz):
    n_seq, length, w = a.shape
    if dil == 1:
        return a
    return jnp.transpose(a.reshape(bsz, dil, length, w), (0, 2, 1, 3)).reshape(
        bsz, length * dil, w)


def _attn_layer(h, g, w_in, q_gain, k_gain, rel_table, w_out):
    bsz = h.shape[0]
    qkv, z = _attn_proj(h, g, w_in, q_gain, k_gain)
    outs, lses = [], []
    for grp, (window, dil) in enumerate(DIL_PAIRS):
        assert (window // 2) // dil == HALF_WINDOW
        q, k, v = (_to_classes(qkv[3 * grp + t], dil) for t in range(3))
        o, lse = _band_attention(q, k, v, _band_bias(rel_table, grp, dil))
        outs.append(_from_classes(o, dil, bsz))
        lses.append(_from_classes(lse, dil, bsz))
    return _attn_out(h, outs, lses, z, w_out)


def kernel(x, norm_g, conv_w_in, conv_kernel, conv_bias, conv_w_out, attn_w_in, q_norm_g,
           k_norm_g, attn_w_out, rel_bias_table):
    h = x
    for layer in range(norm_g.shape[0]):
        j = layer // 2
        if layer % 2 == 0:
            h = _conv_layer(h, norm_g[layer], conv_w_in[j], conv_kernel[j], conv_bias[j],
                            conv_w_out[j])
        else:
            h = _attn_layer(h, norm_g[layer], attn_w_in[j], q_norm_g[j], k_norm_g[j],
                            rel_bias_table, attn_w_out[j])
    return h
```

```python
import functools

import jax
import jax.numpy as jnp
import numpy as np
from jax import lax
from jax.experimental import pallas as pl
from jax.experimental.pallas import tpu as pltpu

EPS = 1e-6
HEAD_DIM = 64
HEADS_PER_GROUP = 8
GROUP_WIDTH = HEAD_DIM * HEADS_PER_GROUP
DIL_PAIRS = ((128, 1), (512, 4), (2048, 16))
N_GROUPS = len(DIL_PAIRS)
HALF_WINDOW = 64
REL_BUCKETS = 32
REL_MAX_DIST = 1024
MASKED = -1e30

LANES = 128
MXU_DIM = 256
Q_BLOCK = 128
K_WINDOW = Q_BLOCK + 2 * HALF_WINDOW
TILE = 512
ATTN_TILES = 4
VMEM_LIMIT_BYTES = 56 * 1024 * 1024

BF16 = jnp.bfloat16
F32 = jnp.float32


def _silu(z):
    return z / (1.0 + jnp.exp(-z))


def _rmsnorm(x, g):
    ms = jnp.mean(x * x, axis=-1, keepdims=True)
    return x * lax.rsqrt(ms + EPS) * g


def _resident(shape):
    zeros = (0,) * len(shape)
    return pl.BlockSpec(shape, lambda *_: zeros, pipeline_mode=pl.Buffered(1))


def _params(n_grid_axes):
    return pltpu.CompilerParams(dimension_semantics=("arbitrary",) * n_grid_axes,
                                vmem_limit_bytes=VMEM_LIMIT_BYTES)


CONV_TOKENS = 512
CONV_HALO = 16
CONV_CHUNK = 512


def _conv_layer_kernel(xp_ref, x_ref, xn_ref, g_ref, wcu_ref, wbz_ref, cw_ref, cb_ref, wo_ref,
                       o_ref, *, n_chunks):
    i = pl.program_id(1)
    last = pl.num_programs(1) - 1
    tm, halo, ce = CONV_TOKENS, CONV_HALO, CONV_CHUNK
    g = g_ref[...]
    x = x_ref[0]
    xp = jnp.where(i > 0, xp_ref[0], 0.0)
    xn = jnp.where(i < last, xn_ref[0], 0.0)
    hn = _rmsnorm(x, g).astype(BF16)
    hne = jnp.concatenate(
        [_rmsnorm(xp, g).astype(BF16), hn, _rmsnorm(xn, g).astype(BF16)], axis=0)
    o_ref[0] = x

    def chunk(j, carry):
        cu = jnp.dot(hne, wcu_ref[j], preferred_element_type=F32)
        bz = jnp.dot(hn, wbz_ref[j], preferred_element_type=F32)
        p = cu[:, :ce] * cu[:, ce:]
        cw = cw_ref[j]
        conv = (cw[0:1] * p[halo - 1:halo - 1 + tm] + cw[1:2] * p[halo:halo + tm]
                + cw[2:3] * p[halo + 1:halo + 1 + tm] + cb_ref[j])
        y = bz[:, :ce] * conv * _silu(bz[:, ce:])
        o_ref[0] += jnp.dot(y.astype(BF16), wo_ref[j], preferred_element_type=F32)
        return carry

    lax.fori_loop(0, n_chunks, chunk, 0, unroll=True)


def _conv_layer(x, g, w_in, conv_w, conv_b, w_out):
    bsz, s, d = x.shape
    e = w_out.shape[0]
    tm, halo, ce = CONV_TOKENS, CONV_HALO, CONV_CHUNK
    assert s % tm == 0 and tm % halo == 0 and e % ce == 0 and w_in.shape == (d, 4 * e)
    n_chunks = e // ce
    w4 = w_in.astype(BF16).reshape(d, 4, n_chunks, ce)
    wcu = jnp.transpose(w4[:, 1:3], (2, 0, 1, 3)).reshape(n_chunks, d, 2 * ce)
    wbz = jnp.transpose(w4[:, 0::3], (2, 0, 1, 3)).reshape(n_chunks, d, 2 * ce)
    cw = jnp.transpose(conv_w.reshape(3, n_chunks, ce), (1, 0, 2))
    cb = conv_b.reshape(n_chunks, 1, ce)
    wo = w_out.astype(BF16).reshape(n_chunks, ce, d)
    per_tile = tm // halo
    n_halo_blocks = s // halo
    return pl.pallas_call(
        functools.partial(_conv_layer_kernel, n_chunks=n_chunks),
        name="conv_layer",
        grid=(bsz, s // tm),
        in_specs=[
            pl.BlockSpec((1, halo, d), lambda b, i: (b, jnp.maximum(i * per_tile - 1, 0), 0)),
            pl.BlockSpec((1, tm, d), lambda b, i: (b, i, 0)),
            pl.BlockSpec((1, halo, d),
                         lambda b, i: (b, jnp.minimum((i + 1) * per_tile, n_halo_blocks - 1), 0)),
            _resident((1, d)),
            _resident((n_chunks, d, 2 * ce)),
            _resident((n_chunks, d, 2 * ce)),
            _resident((n_chunks, 3, ce)),
            _resident((n_chunks, 1, ce)),
            _resident((n_chunks, ce, d)),
        ],
        out_specs=pl.BlockSpec((1, tm, d), lambda b, i: (b, i, 0)),
        out_shape=jax.ShapeDtypeStruct(x.shape, x.dtype),
        compiler_params=_params(2),
    )(x, x, x, g.reshape(1, d), wcu, wbz, cw, cb, wo)


def _attn_proj_kernel(h_ref, g_ref, wqkv_ref, wz_ref, gain_ref, headmean_ref, qkv_ref, z_ref,
                      slab_ref):
    gw, tm = GROUP_WIDTH, TILE
    n_slabs = slab_ref.shape[0]
    hn32 = _rmsnorm(h_ref[0], g_ref[...])
    hn = hn32.astype(BF16)
    for c in range(n_slabs):
        slab_ref[c] = hn32[:, c * LANES:(c + 1) * LANES]
    for grp, (_, dil) in enumerate(DIL_PAIRS):
        if dil == 1:
            hg = hn
        else:
            hg = jnp.concatenate(
                [jnp.concatenate([slab_ref[c, pl.ds(r, tm // dil, stride=dil), :]
                                  for r in range(dil)], axis=0) for c in range(n_slabs)],
                axis=1).astype(BF16)
        for t in range(3):
            c = 3 * grp + t
            y = jnp.dot(hg, wqkv_ref[c], preferred_element_type=F32)
            if t < 2:
                sq = (y * y).astype(BF16)
                ms = jnp.concatenate(
                    [jnp.dot(sq[:, k:k + MXU_DIM], headmean_ref[...], preferred_element_type=F32)
                     for k in range(0, gw, MXU_DIM)], axis=1)
                y = y * lax.rsqrt(ms + EPS) * gain_ref[c]
            qkv_ref[c, 0] = y.astype(BF16)
    for c in range(N_GROUPS):
        z = jnp.dot(hn, wz_ref[:, c * gw:(c + 1) * gw], preferred_element_type=F32)
        z_ref[0, :, c * gw:(c + 1) * gw] = z.astype(BF16)


def _attn_proj(h, g, w_in, q_gain, k_gain):
    bsz, s, d = h.shape
    gw, tm = GROUP_WIDTH, TILE
    n_qkv = 3 * N_GROUPS
    assert s % tm == 0 and d % LANES == 0 and w_in.shape == (d, (n_qkv + N_GROUPS) * gw)
    wb = w_in.astype(BF16)
    wqkv = jnp.transpose(wb[:, :n_qkv * gw].reshape(d, n_qkv, gw), (1, 0, 2))
    wz = wb[:, n_qkv * gw:]
    gains = jnp.stack([q_gain.reshape(N_GROUPS, gw) * HEAD_DIM ** -0.5,
                       k_gain.reshape(N_GROUPS, gw),
                       jnp.ones((N_GROUPS, gw), F32)], axis=1).reshape(n_qkv, 1, gw)
    head = np.arange(MXU_DIM) // HEAD_DIM
    headmean = jnp.asarray((head[:, None] == head[None, :]) / HEAD_DIM, BF16)
    return pl.pallas_call(
        _attn_proj_kernel,
        name="attn_proj",
        grid=(bsz, s // tm),
        in_specs=[
            pl.BlockSpec((1, tm, d), lambda b, i: (b, i, 0)),
            _resident((1, d)),
            _resident((n_qkv, d, gw)),
            _resident((d, N_GROUPS * gw)),
            _resident((n_qkv, 1, gw)),
            _resident((MXU_DIM, MXU_DIM)),
        ],
        out_specs=[
            pl.BlockSpec((n_qkv, 1, tm, gw), lambda b, i: (0, b, i, 0)),
            pl.BlockSpec((1, tm, N_GROUPS * gw), lambda b, i: (b, i, 0)),
        ],
        out_shape=[
            jax.ShapeDtypeStruct((n_qkv, bsz, s, gw), BF16),
            jax.ShapeDtypeStruct((bsz, s, N_GROUPS * gw), BF16),
        ],
        scratch_shapes=[pltpu.VMEM((d // LANES, tm, LANES), F32)],
        compiler_params=_params(2),
    )(h, g.reshape(1, d), wqkv, wz, gains, headmean)


def _band_attn_kernel(q_ref, kp_ref, kc_ref, kn_ref, vp_ref, vc_ref, vn_ref, bias_ref,
                      headmask_ref, o_ref, m_ref, l_ref, kwin_ref, vwin_ref, *, dil, rows):
    hw, qb, tiles = HALF_WINDOW, Q_BLOCK, ATTN_TILES
    seg = tiles * rows
    blocks_per_class = seg // qb
    step, last_step = pl.program_id(1), pl.num_programs(1) - 1

    for win_ref, prev_ref, cur_ref, next_ref in ((kwin_ref, kp_ref, kc_ref, kn_ref),
                                                 (vwin_ref, vp_ref, vc_ref, vn_ref)):
        halo_rows = prev_ref.shape[2]
        for r in range(dil):
            for t in range(prev_ref.shape[0]):
                win_ref[r, t * halo_rows:(t + 1) * halo_rows] = prev_ref[t, r]
                win_ref[r, hw + seg + t * halo_rows:hw + seg + (t + 1) * halo_rows] = (
                    next_ref[t, r])
            for t in range(tiles):
                win_ref[r, hw + t * rows:hw + (t + 1) * rows] = cur_ref[t, r]

    lane = lax.broadcasted_iota(jnp.int32, (qb, LANES), 1)
    low_half = lane < HEAD_DIM

    def block(n, carry):
        r = n // blocks_per_class
        u = n % blocks_per_class
        first = jnp.logical_and(step == 0, u == 0)
        final = jnp.logical_and(step == last_step, u == blocks_per_class - 1)
        variant = jnp.where(first, 0, jnp.where(final, 2, 1))
        start = pl.multiple_of(u * qb, qb)
        if rows >= qb:
            tile = start // rows
            off = pl.multiple_of(start % rows, qb)
            q = q_ref[tile, r, pl.ds(off, qb), :]
        else:
            q = q_ref[:, r].reshape(qb, GROUP_WIDTH)
        m_tile = jnp.zeros((qb, LANES), F32)
        l_tile = jnp.ones((qb, LANES), F32)
        for pair in range(HEADS_PER_GROUP // 2):
            cols = slice(pair * LANES, (pair + 1) * LANES)
            q2 = q[:, cols]
            qq = jnp.concatenate([q2 * headmask_ref[0], q2 * headmask_ref[1]], axis=0)
            kw = kwin_ref[r, pl.ds(start, K_WINDOW), cols]
            vw = vwin_ref[r, pl.ds(start, K_WINDOW), cols]
            s = lax.dot_general(qq, kw, (((1,), (1,)), ((), ())), preferred_element_type=F32)
            s = s + bias_ref[variant, pair]
            m = jnp.max(s, axis=-1, keepdims=True)
            e = jnp.exp(s - m)
            l = jnp.sum(e, axis=-1, keepdims=True)
            o2 = jnp.dot(e.astype(BF16), vw, preferred_element_type=F32)
            o = jnp.where(low_half, o2[:qb], o2[qb:]).astype(BF16)
            if rows >= qb:
                o_ref[tile, r, pl.ds(off, qb), cols] = o
            else:
                o_ref[:, r, :, cols] = o.reshape(tiles, rows, LANES)
            for k in range(2):
                hit = lane == 2 * pair + k
                m_tile = jnp.where(hit, m[k * qb:(k + 1) * qb], m_tile)
                l_tile = jnp.where(hit, l[k * qb:(k + 1) * qb], l_tile)
        if rows >= qb:
            m_ref[tile, r, pl.ds(off, qb), :] = m_tile
            l_ref[tile, r, pl.ds(off, qb), :] = l_tile
        else:
            m_ref[:, r] = m_tile.reshape(tiles, rows, LANES)
            l_ref[:, r] = l_tile.reshape(tiles, rows, LANES)
        return carry

    lax.fori_loop(0, dil * blocks_per_class, block, 0)


def _t5_bucket(rel):
    nb = REL_BUCKETS // 2
    ret = (rel > 0).astype(np.int32) * nb
    n = np.abs(rel)
    max_exact = nb // 2
    large = max_exact + (np.log(np.maximum(n, 1) / max_exact)
                         / np.log(REL_MAX_DIST / max_exact) * (nb - max_exact)).astype(np.int32)
    large = np.minimum(large, nb - 1)
    return ret + np.where(n < max_exact, n, large).astype(np.int32)


def _band_bias(rel_table, group, dil):
    hw, qb, kw = HALF_WINDOW, Q_BLOCK, K_WINDOW
    n_heads = HEADS_PER_GROUP
    buckets = _t5_bucket(np.arange(-hw, hw + 1) * dil)
    heads = slice(group * n_heads, (group + 1) * n_heads)
    per_offset = rel_table[buckets][:, heads].astype(F32).T
    width = qb + kw
    pad = jnp.full((n_heads, qb - 1), MASKED, F32)
    vec = jnp.concatenate([pad, per_offset, pad, jnp.full((n_heads, 2), MASKED, F32)], axis=1)
    assert vec.shape[1] == width + 1
    skew = jnp.tile(vec, (1, qb))[:, :qb * width].reshape(n_heads, qb, width)
    mid = skew[:, :, qb - 1:qb - 1 + kw]
    kj = np.arange(kw)[None, None, :]
    first = jnp.where(kj >= hw, mid, MASKED)
    final = jnp.where(kj < hw + qb, mid, MASKED)
    return jnp.stack([first, mid, final]).reshape(3, n_heads // 2, 2 * qb, kw)


def _band_attention(qkv, group, dil, bias):
    n_qkv, bsz, s, gw = qkv.shape
    hw, qb, tiles = HALF_WINDOW, Q_BLOCK, ATTN_TILES
    rows = TILE // dil
    n_tiles = s // TILE
    n_steps = n_tiles // tiles
    assert s % (TILE * tiles) == 0 and (tiles * rows) % qb == 0 and n_steps >= 2
    assert rows % hw == 0 or hw % rows == 0
    view = qkv.reshape(n_qkv, bsz, n_tiles, dil, rows, gw)
    headmask = jnp.asarray(
        (np.arange(LANES)[None, :] // HEAD_DIM == np.arange(2)[:, None])[:, None, :], BF16)
    if rows >= hw:
        halo_block = (None, None, 1, dil, hw, gw)
        sub = rows // hw
        prev_idx = lambda i: (jnp.maximum(tiles * i - 1, 0), 0, sub - 1, 0)
        next_idx = lambda i: (jnp.minimum(tiles * (i + 1), n_tiles - 1), 0, 0, 0)
    else:
        per = hw // rows
        halo_block = (None, None, per, dil, rows, gw)
        prev_idx = lambda i: (jnp.maximum(tiles // per * i - 1, 0), 0, 0, 0)
        next_idx = lambda i: (jnp.minimum(tiles // per * (i + 1), n_tiles // per - 1), 0, 0, 0)

    def specs(c):
        cur = pl.BlockSpec((None, None, tiles, dil, rows, gw), lambda b, i: (c, b, i, 0, 0, 0))
        prev = pl.BlockSpec(halo_block, lambda b, i: (c, b) + prev_idx(i))
        nxt = pl.BlockSpec(halo_block, lambda b, i: (c, b) + next_idx(i))
        return prev, cur, nxt

    (_, q_spec, _), k_specs, v_specs = specs(3 * group), specs(3 * group + 1), specs(3 * group + 2)
    out_block = lambda width: pl.BlockSpec((None, tiles, dil, rows, width),
                                           lambda b, i: (b, i, 0, 0, 0))
    stat = jax.ShapeDtypeStruct((bsz, n_tiles, dil, rows, LANES), F32)
    win = pltpu.VMEM((dil, tiles * rows + 2 * hw, gw), BF16)
    o, m, l = pl.pallas_call(
        functools.partial(_band_attn_kernel, dil=dil, rows=rows),
        name=f"band_attn_d{dil}",
        grid=(bsz, n_steps),
        in_specs=[q_spec, *k_specs, *v_specs, _resident(bias.shape), _resident((2, 1, LANES))],
        out_specs=[out_block(gw), out_block(LANES), out_block(LANES)],
        out_shape=[jax.ShapeDtypeStruct((bsz, n_tiles, dil, rows, gw), BF16), stat, stat],
        scratch_shapes=[win, win],
        compiler_params=_params(2),
    )(view, view, view, view, view, view, view, bias, headmask)
    return o.reshape(bsz, s, gw), m.reshape(bsz, s, LANES), l.reshape(bsz, s, LANES)


def _attn_out_kernel(h_ref, o0_ref, o1_ref, o2_ref, m0_ref, m1_ref, m2_ref, l0_ref, l1_ref,
                     l2_ref, z_ref, expand_ref, wo_ref, out_ref, slab_ref):
    gw, tm = GROUP_WIDTH, TILE

    def to_token_order(x, dil):
        if dil == 1:
            return x
        rows = tm // dil
        n = x.shape[1] // LANES
        for c in range(n):
            for r in range(dil):
                slab_ref[c, pl.ds(r, rows, stride=dil), :] = (
                    x[r * rows:(r + 1) * rows, c * LANES:(c + 1) * LANES])
        return jnp.concatenate([slab_ref[c] for c in range(n)], axis=1)

    dils = [dil for _, dil in DIL_PAIRS]
    ms = [to_token_order(ref[0], dil) for ref, dil in zip((m0_ref, m1_ref, m2_ref), dils)]
    ls = [to_token_order(ref[0], dil) for ref, dil in zip((l0_ref, l1_ref, l2_ref), dils)]
    top = jnp.maximum(jnp.maximum(ms[0], ms[1]), ms[2])
    ws = [jnp.exp(m - top) for m in ms]
    denom = ls[0] * ws[0] + ls[1] * ws[1] + ls[2] * ws[2]
    acc = h_ref[0]
    for g, o_ref in enumerate((o0_ref, o1_ref, o2_ref)):
        scale = ws[g] / denom
        hi = scale.astype(BF16)
        lo = (scale - hi.astype(F32)).astype(BF16)
        spread = (jnp.dot(hi, expand_ref[...], preferred_element_type=F32)
                  + jnp.dot(lo, expand_ref[...], preferred_element_type=F32))
        z = z_ref[0, :, g * gw:(g + 1) * gw].astype(F32)
        y = to_token_order(o_ref[0].astype(F32), dils[g]) * spread * _silu(z)
        acc = acc + jnp.dot(y.astype(BF16), wo_ref[g], preferred_element_type=F32)
    out_ref[0] = acc


def _attn_out(h, outs, maxes, sums, z, w_out):
    bsz, s, d = h.shape
    gw, tm = GROUP_WIDTH, TILE
    assert s % tm == 0 and w_out.shape == (N_GROUPS * gw, d)
    expand = jnp.asarray(np.arange(LANES)[:, None] == np.arange(gw)[None, :] // HEAD_DIM, BF16)
    tok = lambda width: pl.BlockSpec((1, tm, width), lambda b, i: (b, i, 0))
    return pl.pallas_call(
        _attn_out_kernel,
        name="attn_out",
        grid=(bsz, s // tm),
        in_specs=[tok(d)] + [tok(gw)] * 3 + [tok(LANES)] * 6 + [
            tok(N_GROUPS * gw), _resident((LANES, gw)), _resident((N_GROUPS, gw, d))],
        out_specs=tok(d),
        out_shape=jax.ShapeDtypeStruct(h.shape, h.dtype),
        scratch_shapes=[pltpu.VMEM((gw // LANES, tm, LANES), F32)],
        compiler_params=_params(2),
    )(h, *outs, *maxes, *sums, z, expand, w_out.astype(BF16).reshape(N_GROUPS, gw, d))


def _attn_layer(h, g, w_in, q_gain, k_gain, rel_table, w_out):
    qkv, z = _attn_proj(h, g, w_in, q_gain, k_gain)
    outs, maxes, sums = [], [], []
    for grp, (window, dil) in enumerate(DIL_PAIRS):
        assert (window // 2) // dil == HALF_WINDOW
        o, m, l = _band_attention(qkv, grp, dil, _band_bias(rel_table, grp, dil))
        outs.append(o)
        maxes.append(m)
        sums.append(l)
    return _attn_out(h, outs, maxes, sums, z, w_out)


def kernel(x, norm_g, conv_w_in, conv_kernel, conv_bias, conv_w_out, attn_w_in, q_norm_g,
           k_norm_g, attn_w_out, rel_bias_table):
    h = x
    for layer in range(norm_g.shape[0]):
        j = layer // 2
        if layer % 2 == 0:
            h = _conv_layer(h, norm_g[layer], conv_w_in[j], conv_kernel[j], conv_bias[j],
                            conv_w_out[j])
        else:
            h = _attn_layer(h, norm_g[layer], attn_w_in[j], q_norm_g[j], k_norm_g[j],
                            rel_bias_table, attn_w_out[j])
    return h
```

```python
import functools

import jax
import jax.numpy as jnp
import numpy as np
from jax import lax
from jax.experimental import pallas as pl
from jax.experimental.pallas import tpu as pltpu

EPS = 1e-6
HEAD_DIM = 64
HEADS_PER_GROUP = 8
GROUP_WIDTH = HEAD_DIM * HEADS_PER_GROUP
DIL_PAIRS = ((128, 1), (512, 4), (2048, 16))
N_GROUPS = len(DIL_PAIRS)
HALF_WINDOW = 64
REL_BUCKETS = 32
REL_MAX_DIST = 1024
MASKED = -1e30
LOG2_E = 1.4426950408889634

LANES = 128
MXU_DIM = 256
Q_BLOCK = 128
K_WINDOW = Q_BLOCK + 2 * HALF_WINDOW
TILE = 512
ATTN_TILES = 4
VMEM_LIMIT_BYTES = 56 * 1024 * 1024

BF16 = jnp.bfloat16
F32 = jnp.float32


def _silu(z):
    return z / (1.0 + jnp.exp(-z))


def _rmsnorm(x, g):
    ms = jnp.mean(x * x, axis=-1, keepdims=True)
    return x * lax.rsqrt(ms + EPS) * g


def _resident(shape):
    zeros = (0,) * len(shape)
    return pl.BlockSpec(shape, lambda *_: zeros, pipeline_mode=pl.Buffered(1))


def _params(n_grid_axes):
    return pltpu.CompilerParams(dimension_semantics=("arbitrary",) * n_grid_axes,
                                vmem_limit_bytes=VMEM_LIMIT_BYTES)


CONV_TOKENS = 512
CONV_HALO = 16
CONV_CHUNK = 512


def _conv_layer_kernel(xp_ref, x_ref, xn_ref, g_ref, w_ref, cw_ref, cb_ref, wo_ref, o_ref):
    i = pl.program_id(1)
    last = pl.num_programs(1) - 1
    tm, halo, ce = CONV_TOKENS, CONV_HALO, CONV_CHUNK
    e = wo_ref.shape[0]
    g = g_ref[...]
    x = x_ref[0]
    xp = jnp.where(i > 0, xp_ref[0], 0.0)
    xn = jnp.where(i < last, xn_ref[0], 0.0)
    hn = _rmsnorm(x, g).astype(BF16)
    hne = jnp.concatenate(
        [_rmsnorm(xp, g).astype(BF16), hn, _rmsnorm(xn, g).astype(BF16)], axis=0)
    o_ref[0] = x

    for j in range(e // ce):
        ch = slice(j * ce, (j + 1) * ce)

        def proj(lhs, part):
            return jnp.dot(lhs, w_ref[:, part * e + j * ce:part * e + (j + 1) * ce],
                           preferred_element_type=F32)

        p = proj(hne, 1) * proj(hne, 2)
        conv = (cw_ref[0:1, ch] * p[halo - 1:halo - 1 + tm] + cw_ref[1:2, ch] * p[halo:halo + tm]
                + cw_ref[2:3, ch] * p[halo + 1:halo + 1 + tm] + cb_ref[:, ch])
        y = proj(hn, 0) * conv * _silu(proj(hn, 3))
        o_ref[0] += jnp.dot(y.astype(BF16), wo_ref[ch, :], preferred_element_type=F32)


def _conv_layer(x, g, w_in, conv_w, conv_b, w_out):
    bsz, s, d = x.shape
    e = w_out.shape[0]
    tm, halo, ce = CONV_TOKENS, CONV_HALO, CONV_CHUNK
    assert s % tm == 0 and tm % halo == 0 and e % ce == 0 and w_in.shape == (d, 4 * e)
    per_tile = tm // halo
    n_halo_blocks = s // halo
    return pl.pallas_call(
        _conv_layer_kernel,
        name="conv_layer",
        grid=(bsz, s // tm),
        in_specs=[
            pl.BlockSpec((1, halo, d), lambda b, i: (b, jnp.maximum(i * per_tile - 1, 0), 0)),
            pl.BlockSpec((1, tm, d), lambda b, i: (b, i, 0)),
            pl.BlockSpec((1, halo, d),
                         lambda b, i: (b, jnp.minimum((i + 1) * per_tile, n_halo_blocks - 1), 0)),
            _resident((1, d)),
            _resident((d, 4 * e)),
            _resident((3, e)),
            _resident((1, e)),
            _resident((e, d)),
        ],
        out_specs=pl.BlockSpec((1, tm, d), lambda b, i: (b, i, 0)),
        out_shape=jax.ShapeDtypeStruct(x.shape, x.dtype),
        compiler_params=_params(2),
    )(x, x, x, g.reshape(1, d), w_in.astype(BF16), conv_w, conv_b.reshape(1, e),
      w_out.astype(BF16))


def _attn_proj_kernel(h_ref, g_ref, w_ref, gain_ref, headmean_ref, qkv_ref, z_ref, slab_ref):
    gw, tm = GROUP_WIDTH, TILE
    n_slabs = slab_ref.shape[0]
    n_qkv = 3 * N_GROUPS
    hn32 = _rmsnorm(h_ref[0], g_ref[...])
    hn = hn32.astype(BF16)
    for c in range(n_slabs):
        slab_ref[c] = hn32[:, c * LANES:(c + 1) * LANES]
    for grp, (_, dil) in enumerate(DIL_PAIRS):
        if dil == 1:
            hg = hn
        else:
            hg = jnp.concatenate(
                [jnp.concatenate([slab_ref[c, pl.ds(r, tm // dil, stride=dil), :]
                                  for r in range(dil)], axis=0) for c in range(n_slabs)],
                axis=1).astype(BF16)
        for t in range(3):
            c = 3 * grp + t
            y = jnp.dot(hg, w_ref[:, c * gw:(c + 1) * gw], preferred_element_type=F32)
            if t < 2:
                sq = (y * y).astype(BF16)
                ms = jnp.concatenate(
                    [jnp.dot(sq[:, k:k + MXU_DIM], headmean_ref[...], preferred_element_type=F32)
                     for k in range(0, gw, MXU_DIM)], axis=1)
                y = y * lax.rsqrt(ms + EPS) * gain_ref[c]
            qkv_ref[c, 0] = y.astype(BF16)
    for c in range(N_GROUPS):
        z = jnp.dot(hn, w_ref[:, (n_qkv + c) * gw:(n_qkv + c + 1) * gw],
                    preferred_element_type=F32)
        z_ref[0, :, c * gw:(c + 1) * gw] = z.astype(BF16)


def _attn_proj(h, g, w_in, q_gain, k_gain):
    bsz, s, d = h.shape
    gw, tm = GROUP_WIDTH, TILE
    n_qkv = 3 * N_GROUPS
    assert s % tm == 0 and d % LANES == 0 and w_in.shape == (d, (n_qkv + N_GROUPS) * gw)
    gains = jnp.stack([q_gain.reshape(N_GROUPS, gw) * (HEAD_DIM ** -0.5 * LOG2_E),
                       k_gain.reshape(N_GROUPS, gw),
                       jnp.ones((N_GROUPS, gw), F32)], axis=1).reshape(n_qkv, 1, gw)
    head = np.arange(MXU_DIM) // HEAD_DIM
    headmean = jnp.asarray((head[:, None] == head[None, :]) / HEAD_DIM, BF16)
    return pl.pallas_call(
        _attn_proj_kernel,
        name="attn_proj",
        grid=(bsz, s // tm),
        in_specs=[
            pl.BlockSpec((1, tm, d), lambda b, i: (b, i, 0)),
            _resident((1, d)),
            _resident(w_in.shape),
            _resident((n_qkv, 1, gw)),
            _resident((MXU_DIM, MXU_DIM)),
        ],
        out_specs=[
            pl.BlockSpec((n_qkv, 1, tm, gw), lambda b, i: (0, b, i, 0)),
            pl.BlockSpec((1, tm, N_GROUPS * gw), lambda b, i: (b, i, 0)),
        ],
        out_shape=[
            jax.ShapeDtypeStruct((n_qkv, bsz, s, gw), BF16),
            jax.ShapeDtypeStruct((bsz, s, N_GROUPS * gw), BF16),
        ],
        scratch_shapes=[pltpu.VMEM((d // LANES, tm, LANES), F32)],
        compiler_params=_params(2),
    )(h, g.reshape(1, d), w_in.astype(BF16), gains, headmean)


def _band_attn_kernel(q_ref, kp_ref, kc_ref, kn_ref, vp_ref, vc_ref, vn_ref, bias_ref,
                      headmask_ref, o_ref, m_ref, l_ref, kwin_ref, vwin_ref, *, dil, rows):
    hw, qb, tiles = HALF_WINDOW, Q_BLOCK, ATTN_TILES
    seg = tiles * rows
    blocks_per_class = seg // qb
    step, last_step = pl.program_id(1), pl.num_programs(1) - 1

    for win_ref, prev_ref, cur_ref, next_ref in ((kwin_ref, kp_ref, kc_ref, kn_ref),
                                                 (vwin_ref, vp_ref, vc_ref, vn_ref)):
        halo_rows = prev_ref.shape[2]
        for r in range(dil):
            for t in range(prev_ref.shape[0]):
                win_ref[r, t * halo_rows:(t + 1) * halo_rows] = prev_ref[t, r]
                win_ref[r, hw + seg + t * halo_rows:hw + seg + (t + 1) * halo_rows] = (
                    next_ref[t, r])
            for t in range(tiles):
                win_ref[r, hw + t * rows:hw + (t + 1) * rows] = cur_ref[t, r]

    lane = lax.broadcasted_iota(jnp.int32, (qb, LANES), 1)
    low_half = lane < HEAD_DIM

    def block(n, carry):
        r = n // blocks_per_class
        u = n % blocks_per_class
        first = jnp.logical_and(step == 0, u == 0)
        final = jnp.logical_and(step == last_step, u == blocks_per_class - 1)
        variant = jnp.where(first, 0, jnp.where(final, 2, 1))
        start = pl.multiple_of(u * qb, qb)
        if rows >= qb:
            tile = start // rows
            off = pl.multiple_of(start % rows, qb)
            q = q_ref[tile, r, pl.ds(off, qb), :]
        else:
            q = q_ref[:, r].reshape(qb, GROUP_WIDTH)
        m_tile = jnp.zeros((qb, LANES), F32)
        l_tile = jnp.ones((qb, LANES), F32)
        for pair in range(HEADS_PER_GROUP // 2):
            cols = slice(pair * LANES, (pair + 1) * LANES)
            q2 = q[:, cols]
            qq = jnp.concatenate([q2 * headmask_ref[0], q2 * headmask_ref[1]], axis=0)
            kw = kwin_ref[r, pl.ds(start, K_WINDOW), cols]
            vw = jnp.concatenate([vwin_ref[r, pl.ds(start, K_WINDOW), cols],
                                  jnp.ones((K_WINDOW, LANES), BF16)], axis=1)
            s = lax.dot_general(qq, kw, (((1,), (1,)), ((), ())), preferred_element_type=F32)
            s = s + bias_ref[variant, pair]
            m = jnp.max(s, axis=-1, keepdims=True)
            e = jnp.exp2(s - m)
            o2 = jnp.dot(e.astype(BF16), vw, preferred_element_type=F32)
            l = o2[:, LANES:]
            o = jnp.where(low_half, o2[:qb, :LANES], o2[qb:, :LANES]).astype(BF16)
            if rows >= qb:
                o_ref[tile, r, pl.ds(off, qb), cols] = o
            else:
                o_ref[:, r, :, cols] = o.reshape(tiles, rows, LANES)
            for k in range(2):
                hit = lane == 2 * pair + k
                m_tile = jnp.where(hit, m[k * qb:(k + 1) * qb], m_tile)
                l_tile = jnp.where(hit, l[k * qb:(k + 1) * qb], l_tile)
        if rows >= qb:
            m_ref[tile, r, pl.ds(off, qb), :] = m_tile
            l_ref[tile, r, pl.ds(off, qb), :] = l_tile
        else:
            m_ref[:, r] = m_tile.reshape(tiles, rows, LANES)
            l_ref[:, r] = l_tile.reshape(tiles, rows, LANES)
        return carry

    lax.fori_loop(0, dil * blocks_per_class, block, 0, unroll=8)


def _t5_bucket(rel):
    nb = REL_BUCKETS // 2
    ret = (rel > 0).astype(np.int32) * nb
    n = np.abs(rel)
    max_exact = nb // 2
    large = max_exact + (np.log(np.maximum(n, 1) / max_exact)
                         / np.log(REL_MAX_DIST / max_exact) * (nb - max_exact)).astype(np.int32)
    large = np.minimum(large, nb - 1)
    return ret + np.where(n < max_exact, n, large).astype(np.int32)


def _band_bias(rel_table, group, dil):
    hw, qb, kw = HALF_WINDOW, Q_BLOCK, K_WINDOW
    n_heads = HEADS_PER_GROUP
    buckets = _t5_bucket(np.arange(-hw, hw + 1) * dil)
    heads = slice(group * n_heads, (group + 1) * n_heads)
    per_offset = rel_table[buckets][:, heads].astype(F32).T * LOG2_E
    width = qb + kw
    pad = jnp.full((n_heads, qb - 1), MASKED, F32)
    vec = jnp.concatenate([pad, per_offset, pad, jnp.full((n_heads, 2), MASKED, F32)], axis=1)
    assert vec.shape[1] == width + 1
    skew = jnp.tile(vec, (1, qb))[:, :qb * width].reshape(n_heads, qb, width)
    mid = skew[:, :, qb - 1:qb - 1 + kw]
    kj = np.arange(kw)[None, None, :]
    first = jnp.where(kj >= hw, mid, MASKED)
    final = jnp.where(kj < hw + qb, mid, MASKED)
    return jnp.stack([first, mid, final]).reshape(3, n_heads // 2, 2 * qb, kw)


def _band_attention(qkv, group, dil, bias):
    n_qkv, bsz, s, gw = qkv.shape
    hw, qb, tiles = HALF_WINDOW, Q_BLOCK, ATTN_TILES
    rows = TILE // dil
    n_tiles = s // TILE
    n_steps = n_tiles // tiles
    assert s % (TILE * tiles) == 0 and (tiles * rows) % qb == 0 and n_steps >= 2
    assert rows % hw == 0 or hw % rows == 0
    view = qkv.reshape(n_qkv, bsz, n_tiles, dil, rows, gw)
    headmask = jnp.asarray(
        (np.arange(LANES)[None, :] // HEAD_DIM == np.arange(2)[:, None])[:, None, :], BF16)
    if rows >= hw:
        halo_block = (None, None, 1, dil, hw, gw)
        sub = rows // hw
        prev_idx = lambda i: (jnp.maximum(tiles * i - 1, 0), 0, sub - 1, 0)
        next_idx = lambda i: (jnp.minimum(tiles * (i + 1), n_tiles - 1), 0, 0, 0)
    else:
        per = hw // rows
        halo_block = (None, None, per, dil, rows, gw)
        prev_idx = lambda i: (jnp.maximum(tiles // per * i - 1, 0), 0, 0, 0)
        next_idx = lambda i: (jnp.minimum(tiles // per * (i + 1), n_tiles // per - 1), 0, 0, 0)

    def specs(c):
        cur = pl.BlockSpec((None, None, tiles, dil, rows, gw), lambda b, i: (c, b, i, 0, 0, 0))
        prev = pl.BlockSpec(halo_block, lambda b, i: (c, b) + prev_idx(i))
        nxt = pl.BlockSpec(halo_block, lambda b, i: (c, b) + next_idx(i))
        return prev, cur, nxt

    (_, q_spec, _), k_specs, v_specs = specs(3 * group), specs(3 * group + 1), specs(3 * group + 2)
    out_block = lambda width: pl.BlockSpec((None, tiles, dil, rows, width),
                                           lambda b, i: (b, i, 0, 0, 0))
    stat = jax.ShapeDtypeStruct((bsz, n_tiles, dil, rows, LANES), F32)
    win = pltpu.VMEM((dil, tiles * rows + 2 * hw, gw), BF16)
    o, m, l = pl.pallas_call(
        functools.partial(_band_attn_kernel, dil=dil, rows=rows),
        name=f"band_attn_d{dil}",
        grid=(bsz, n_steps),
        in_specs=[q_spec, *k_specs, *v_specs, _resident(bias.shape), _resident((2, 1, LANES))],
        out_specs=[out_block(gw), out_block(LANES), out_block(LANES)],
        out_shape=[jax.ShapeDtypeStruct((bsz, n_tiles, dil, rows, gw), BF16), stat, stat],
        scratch_shapes=[win, win],
        compiler_params=_params(2),
    )(view, view, view, view, view, view, view, bias, headmask)
    return o.reshape(bsz, s, gw), m.reshape(bsz, s, LANES), l.reshape(bsz, s, LANES)


def _attn_out_kernel(h_ref, o0_ref, o1_ref, o2_ref, m0_ref, m1_ref, m2_ref, l0_ref, l1_ref,
                     l2_ref, z_ref, expand_ref, wo_ref, out_ref, slab_ref):
    gw, tm = GROUP_WIDTH, TILE

    def to_token_order(x, dil):
        if dil == 1:
            return x
        rows = tm // dil
        n = x.shape[1] // LANES
        for c in range(n):
            for r in range(dil):
                slab_ref[c, pl.ds(r, rows, stride=dil), :] = (
                    x[r * rows:(r + 1) * rows, c * LANES:(c + 1) * LANES])
        return jnp.concatenate([slab_ref[c] for c in range(n)], axis=1)

    dils = [dil for _, dil in DIL_PAIRS]
    ms = [to_token_order(ref[0], dil) for ref, dil in zip((m0_ref, m1_ref, m2_ref), dils)]
    ls = [to_token_order(ref[0], dil) for ref, dil in zip((l0_ref, l1_ref, l2_ref), dils)]
    top = jnp.maximum(jnp.maximum(ms[0], ms[1]), ms[2])
    ws = [jnp.exp2(m - top) for m in ms]
    denom = ls[0] * ws[0] + ls[1] * ws[1] + ls[2] * ws[2]
    acc = h_ref[0]
    for g, o_ref in enumerate((o0_ref, o1_ref, o2_ref)):
        scale = ws[g] / denom
        hi = scale.astype(BF16)
        lo = (scale - hi.astype(F32)).astype(BF16)
        spread = (jnp.dot(hi, expand_ref[...], preferred_element_type=F32)
                  + jnp.dot(lo, expand_ref[...], preferred_element_type=F32))
        z = z_ref[0, :, g * gw:(g + 1) * gw].astype(F32)
        y = to_token_order(o_ref[0].astype(F32), dils[g]) * spread * _silu(z)
        acc = acc + jnp.dot(y.astype(BF16), wo_ref[g], preferred_element_type=F32)
    out_ref[0] = acc


def _attn_out(h, outs, maxes, sums, z, w_out):
    bsz, s, d = h.shape
    gw, tm = GROUP_WIDTH, TILE
    assert s % tm == 0 and w_out.shape == (N_GROUPS * gw, d)
    expand = jnp.asarray(np.arange(LANES)[:, None] == np.arange(gw)[None, :] // HEAD_DIM, BF16)
    tok = lambda width: pl.BlockSpec((1, tm, width), lambda b, i: (b, i, 0))
    return pl.pallas_call(
        _attn_out_kernel,
        name="attn_out",
        grid=(bsz, s // tm),
        in_specs=[tok(d)] + [tok(gw)] * 3 + [tok(LANES)] * 6 + [
            tok(N_GROUPS * gw), _resident((LANES, gw)), _resident((N_GROUPS, gw, d))],
        out_specs=tok(d),
        out_shape=jax.ShapeDtypeStruct(h.shape, h.dtype),
        scratch_shapes=[pltpu.VMEM((gw // LANES, tm, LANES), F32)],
        compiler_params=_params(2),
    )(h, *outs, *maxes, *sums, z, expand, w_out.astype(BF16).reshape(N_GROUPS, gw, d))


def _attn_layer(h, g, w_in, q_gain, k_gain, rel_table, w_out):
    qkv, z = _attn_proj(h, g, w_in, q_gain, k_gain)
    outs, maxes, sums = [], [], []
    for grp, (window, dil) in enumerate(DIL_PAIRS):
        assert (window // 2) // dil == HALF_WINDOW
        o, m, l = _band_attention(qkv, grp, dil, _band_bias(rel_table, grp, dil))
        outs.append(o)
        maxes.append(m)
        sums.append(l)
    return _attn_out(h, outs, maxes, sums, z, w_out)


def kernel(x, norm_g, conv_w_in, conv_kernel, conv_bias, conv_w_out, attn_w_in, q_norm_g,
           k_norm_g, attn_w_out, rel_bias_table):
    h = x
    for layer in range(norm_g.shape[0]):
        j = layer // 2
        if layer % 2 == 0:
            h = _conv_layer(h, norm_g[layer], conv_w_in[j], conv_kernel[j], conv_bias[j],
                            conv_w_out[j])
        else:
            h = _attn_layer(h, norm_g[layer], attn_w_in[j], q_norm_g[j], k_norm_g[j],
                            rel_bias_table, attn_w_out[j])
    return h
```

```python
import functools

import jax
import jax.numpy as jnp
import numpy as np
from jax import lax
from jax.experimental import pallas as pl
from jax.experimental.pallas import tpu as pltpu

EPS = 1e-6
HEAD_DIM = 64
HEADS_PER_GROUP = 8
GROUP_WIDTH = HEAD_DIM * HEADS_PER_GROUP
DIL_PAIRS = ((128, 1), (512, 4), (2048, 16))
N_GROUPS = len(DIL_PAIRS)
HALF_WINDOW = 64
REL_BUCKETS = 32
REL_MAX_DIST = 1024
MASKED = -1e30
LOG2_E = 1.4426950408889634

LANES = 128
MXU_DIM = 256
Q_BLOCK = 128
K_WINDOW = Q_BLOCK + 2 * HALF_WINDOW
TILE = 512
ATTN_TILES = 4
VMEM_LIMIT_BYTES = 56 * 1024 * 1024

BF16 = jnp.bfloat16
F32 = jnp.float32


def _silu(z):
    return z / (1.0 + jnp.exp(-z))


def _rmsnorm(x, g):
    ms = jnp.mean(x * x, axis=-1, keepdims=True)
    return x * lax.rsqrt(ms + EPS) * g


def _resident(shape):
    zeros = (0,) * len(shape)
    return pl.BlockSpec(shape, lambda *_: zeros, pipeline_mode=pl.Buffered(1))


def _params(n_grid_axes):
    return pltpu.CompilerParams(dimension_semantics=("arbitrary",) * n_grid_axes,
                                vmem_limit_bytes=VMEM_LIMIT_BYTES)


CONV_TOKENS = 512
CONV_HALO = 8
CONV_CHUNK = 512


def _conv_layer_kernel(xp_ref, x_ref, xn_ref, g_ref, w_ref, cw_ref, cb_ref, wo_ref, o_ref):
    i = pl.program_id(1)
    last = pl.num_programs(1) - 1
    tm, halo, ce = CONV_TOKENS, CONV_HALO, CONV_CHUNK
    e = wo_ref.shape[0]
    g = g_ref[...]
    x = x_ref[0]
    xp = jnp.where(i > 0, xp_ref[0], 0.0)
    xn = jnp.where(i < last, xn_ref[0], 0.0)
    hn32 = _rmsnorm(x, g)
    hn = hn32.astype(BF16)
    hne = jnp.concatenate([_rmsnorm(xp, g), hn32, _rmsnorm(xn, g)], axis=0).astype(BF16)
    o_ref[0] = x

    for j in range(e // ce):
        ch = slice(j * ce, (j + 1) * ce)

        def proj(lhs, part):
            return jnp.dot(lhs, w_ref[:, part * e + j * ce:part * e + (j + 1) * ce],
                           preferred_element_type=F32)

        p = proj(hne, 1) * proj(hne, 2)
        conv = (cw_ref[0:1, ch] * p[halo - 1:halo - 1 + tm] + cw_ref[1:2, ch] * p[halo:halo + tm]
                + cw_ref[2:3, ch] * p[halo + 1:halo + 1 + tm] + cb_ref[:, ch])
        y = proj(hn, 0) * conv * _silu(proj(hn, 3))
        o_ref[0] += jnp.dot(y.astype(BF16), wo_ref[ch, :], preferred_element_type=F32)


def _conv_layer(x, g, w_in, conv_w, conv_b, w_out):
    bsz, s, d = x.shape
    e = w_out.shape[0]
    tm, halo, ce = CONV_TOKENS, CONV_HALO, CONV_CHUNK
    assert s % tm == 0 and tm % halo == 0 and e % ce == 0 and w_in.shape == (d, 4 * e)
    per_tile = tm // halo
    n_halo_blocks = s // halo
    return pl.pallas_call(
        _conv_layer_kernel,
        name="conv_layer",
        grid=(bsz, s // tm),
        in_specs=[
            pl.BlockSpec((1, halo, d), lambda b, i: (b, jnp.maximum(i * per_tile - 1, 0), 0)),
            pl.BlockSpec((1, tm, d), lambda b, i: (b, i, 0)),
            pl.BlockSpec((1, halo, d),
                         lambda b, i: (b, jnp.minimum((i + 1) * per_tile, n_halo_blocks - 1), 0)),
            _resident((1, d)),
            _resident((d, 4 * e)),
            _resident((3, e)),
            _resident((1, e)),
            _resident((e, d)),
        ],
        out_specs=pl.BlockSpec((1, tm, d), lambda b, i: (b, i, 0)),
        out_shape=jax.ShapeDtypeStruct(x.shape, x.dtype),
        compiler_params=_params(2),
    )(x, x, x, g.reshape(1, d), w_in.astype(BF16), conv_w, conv_b.reshape(1, e),
      w_out.astype(BF16))


def _attn_proj_kernel(h_ref, g_ref, w_ref, gain_ref, headmean_ref, qkv_ref, z_ref, slab_ref):
    gw, tm = GROUP_WIDTH, TILE
    n_slabs = slab_ref.shape[0]
    n_qkv = 3 * N_GROUPS
    hn32 = _rmsnorm(h_ref[0], g_ref[...])
    hn = hn32.astype(BF16)
    for c in range(n_slabs):
        slab_ref[c] = hn32[:, c * LANES:(c + 1) * LANES]
    for grp, (_, dil) in enumerate(DIL_PAIRS):
        if dil == 1:
            hg = hn
        else:
            hg = jnp.concatenate(
                [jnp.concatenate([slab_ref[c, pl.ds(r, tm // dil, stride=dil), :]
                                  for r in range(dil)], axis=0) for c in range(n_slabs)],
                axis=1).astype(BF16)
        for t in range(3):
            c = 3 * grp + t
            y = jnp.dot(hg, w_ref[:, c * gw:(c + 1) * gw], preferred_element_type=F32)
            if t < 2:
                sq = (y * y).astype(BF16)
                ms = jnp.concatenate(
                    [jnp.dot(sq[:, k:k + MXU_DIM], headmean_ref[...], preferred_element_type=F32)
                     for k in range(0, gw, MXU_DIM)], axis=1)
                y = y * lax.rsqrt(ms + EPS) * gain_ref[c]
            qkv_ref[c, 0] = y.astype(BF16)
    for c in range(N_GROUPS):
        z = jnp.dot(hn, w_ref[:, (n_qkv + c) * gw:(n_qkv + c + 1) * gw],
                    preferred_element_type=F32)
        z_ref[0, :, c * gw:(c + 1) * gw] = z.astype(BF16)


def _attn_proj(h, g, w_in, q_gain, k_gain):
    bsz, s, d = h.shape
    gw, tm = GROUP_WIDTH, TILE
    n_qkv = 3 * N_GROUPS
    assert s % tm == 0 and d % LANES == 0 and w_in.shape == (d, (n_qkv + N_GROUPS) * gw)
    gains = jnp.stack([q_gain.reshape(N_GROUPS, gw) * (HEAD_DIM ** -0.5 * LOG2_E),
                       k_gain.reshape(N_GROUPS, gw),
                       jnp.ones((N_GROUPS, gw), F32)], axis=1).reshape(n_qkv, 1, gw)
    head = np.arange(MXU_DIM) // HEAD_DIM
    headmean = jnp.asarray((head[:, None] == head[None, :]) / HEAD_DIM, BF16)
    return pl.pallas_call(
        _attn_proj_kernel,
        name="attn_proj",
        grid=(bsz, s // tm),
        in_specs=[
            pl.BlockSpec((1, tm, d), lambda b, i: (b, i, 0)),
            _resident((1, d)),
            _resident(w_in.shape),
            _resident((n_qkv, 1, gw)),
            _resident((MXU_DIM, MXU_DIM)),
        ],
        out_specs=[
            pl.BlockSpec((n_qkv, 1, tm, gw), lambda b, i: (0, b, i, 0)),
            pl.BlockSpec((1, tm, N_GROUPS * gw), lambda b, i: (b, i, 0)),
        ],
        out_shape=[
            jax.ShapeDtypeStruct((n_qkv, bsz, s, gw), BF16),
            jax.ShapeDtypeStruct((bsz, s, N_GROUPS * gw), BF16),
        ],
        scratch_shapes=[pltpu.VMEM((d // LANES, tm, LANES), F32)],
        compiler_params=_params(2),
    )(h, g.reshape(1, d), w_in.astype(BF16), gains, headmean)


def _class_row_spans(prev_ref, cur_ref, next_ref, start, stop):
    rows = cur_ref.shape[2]
    seg = cur_ref.shape[0] * rows
    spans, pos = [], start
    while pos < stop:
        if pos < 0:
            ref, base, limit = prev_ref, pos + HALF_WINDOW, 0
        elif pos >= seg:
            ref, base, limit = next_ref, pos - seg, stop
        else:
            ref, base, limit = cur_ref, pos, seg
        tile, first = divmod(base, ref.shape[2])
        n = min(min(stop, limit) - pos, ref.shape[2] - first)
        spans.append((ref, tile, first, n))
        pos += n
    return spans


def _band_attn_kernel(q_ref, kp_ref, kc_ref, kn_ref, vp_ref, vc_ref, vn_ref, bias_ref,
                      headmask_ref, o_ref, stat_ref, *, dil):
    hw, qb = HALF_WINDOW, Q_BLOCK
    n_heads = HEADS_PER_GROUP
    blocks_per_class = q_ref.shape[0] * q_ref.shape[2] // qb
    step, last_step = pl.program_id(1), pl.num_programs(1) - 1
    lane = lax.broadcasted_iota(jnp.int32, (qb, LANES), 1)
    low_half = lane < HEAD_DIM

    def gather(spans, r, cols):
        parts = [ref[tile, r, first:first + n, cols] for ref, tile, first, n in spans]
        return parts[0] if len(parts) == 1 else jnp.concatenate(parts, axis=0)

    def scatter(ref, spans, r, cols, value):
        done = 0
        for _, tile, first, n in spans:
            ref[tile, r, first:first + n, cols] = value[done:done + n]
            done += n

    for r in range(dil):
        for u in range(blocks_per_class):
            q_spans = _class_row_spans(None, q_ref, None, u * qb, (u + 1) * qb)
            k_spans = _class_row_spans(kp_ref, kc_ref, kn_ref, u * qb - hw, (u + 1) * qb + hw)
            v_spans = _class_row_spans(vp_ref, vc_ref, vn_ref, u * qb - hw, (u + 1) * qb + hw)
            variant = 1
            if u == 0:
                variant = jnp.where(step == 0, 0, variant)
            if u == blocks_per_class - 1:
                variant = jnp.where(step == last_step, 2, variant)
            stats = jnp.zeros((qb, LANES), F32)
            for pair in range(n_heads // 2):
                cols = slice(pair * LANES, (pair + 1) * LANES)
                q2 = gather(q_spans, r, cols)
                qq = jnp.concatenate([q2 * headmask_ref[0], q2 * headmask_ref[1]], axis=0)
                kw = gather(k_spans, r, cols)
                vw = jnp.concatenate([gather(v_spans, r, cols),
                                      jnp.ones((K_WINDOW, LANES), BF16)], axis=1)
                s = lax.dot_general(qq, kw, (((1,), (1,)), ((), ())),
                                    preferred_element_type=F32)
                s = s + bias_ref[variant, pair]
                m = jnp.max(s, axis=-1, keepdims=True)
                e = jnp.exp2(s - m)
                o2 = jnp.dot(e.astype(BF16), vw, preferred_element_type=F32)
                l = o2[:, LANES:]
                o = jnp.where(low_half, o2[:qb, :LANES], o2[qb:, :LANES]).astype(BF16)
                scatter(o_ref, q_spans, r, cols, o)
                for k in range(2):
                    head = 2 * pair + k
                    stats = jnp.where(lane == head, m[k * qb:(k + 1) * qb], stats)
                    stats = jnp.where(lane == n_heads + head, l[k * qb:(k + 1) * qb], stats)
            scatter(stat_ref, q_spans, r, slice(None), stats)


def _t5_bucket(rel):
    nb = REL_BUCKETS // 2
    ret = (rel > 0).astype(np.int32) * nb
    n = np.abs(rel)
    max_exact = nb // 2
    large = max_exact + (np.log(np.maximum(n, 1) / max_exact)
                         / np.log(REL_MAX_DIST / max_exact) * (nb - max_exact)).astype(np.int32)
    large = np.minimum(large, nb - 1)
    return ret + np.where(n < max_exact, n, large).astype(np.int32)


def _band_bias(rel_table, group, dil):
    hw, qb, kw = HALF_WINDOW, Q_BLOCK, K_WINDOW
    n_heads = HEADS_PER_GROUP
    buckets = _t5_bucket(np.arange(-hw, hw + 1) * dil)
    heads = slice(group * n_heads, (group + 1) * n_heads)
    per_offset = rel_table[buckets][:, heads].astype(F32).T * LOG2_E
    width = qb + kw
    pad = jnp.full((n_heads, qb - 1), MASKED, F32)
    vec = jnp.concatenate([pad, per_offset, pad, jnp.full((n_heads, 2), MASKED, F32)], axis=1)
    assert vec.shape[1] == width + 1
    skew = jnp.tile(vec, (1, qb))[:, :qb * width].reshape(n_heads, qb, width)
    mid = skew[:, :, qb - 1:qb - 1 + kw]
    kj = np.arange(kw)[None, None, :]
    first = jnp.where(kj >= hw, mid, MASKED)
    final = jnp.where(kj < hw + qb, mid, MASKED)
    return jnp.stack([first, mid, final]).reshape(3, n_heads // 2, 2 * qb, kw)


def _band_attention(qkv, group, dil, bias):
    n_qkv, bsz, s, gw = qkv.shape
    hw, qb, tiles = HALF_WINDOW, Q_BLOCK, ATTN_TILES
    rows = TILE // dil
    n_tiles = s // TILE
    n_steps = n_tiles // tiles
    assert s % (TILE * tiles) == 0 and (tiles * rows) % qb == 0 and n_steps >= 2
    assert rows % hw == 0 or hw % rows == 0
    view = qkv.reshape(n_qkv, bsz, n_tiles, dil, rows, gw)
    headmask = jnp.asarray(
        (np.arange(LANES)[None, :] // HEAD_DIM == np.arange(2)[:, None])[:, None, :], BF16)
    if rows >= hw:
        halo_block = (None, None, 1, dil, hw, gw)
        sub = rows // hw
        prev_idx = lambda i: (jnp.maximum(tiles * i - 1, 0), 0, sub - 1, 0)
        next_idx = lambda i: (jnp.minimum(tiles * (i + 1), n_tiles - 1), 0, 0, 0)
    else:
        per = hw // rows
        halo_block = (None, None, per, dil, rows, gw)
        prev_idx = lambda i: (jnp.maximum(tiles // per * i - 1, 0), 0, 0, 0)
        next_idx = lambda i: (jnp.minimum(tiles // per * (i + 1), n_tiles // per - 1), 0, 0, 0)

    def specs(c):
        cur = pl.BlockSpec((None, None, tiles, dil, rows, gw), lambda b, i: (c, b, i, 0, 0, 0))
        prev = pl.BlockSpec(halo_block, lambda b, i: (c, b) + prev_idx(i))
        nxt = pl.BlockSpec(halo_block, lambda b, i: (c, b) + next_idx(i))
        return prev, cur, nxt

    (_, q_spec, _), k_specs, v_specs = specs(3 * group), specs(3 * group + 1), specs(3 * group + 2)
    out_block = lambda width: pl.BlockSpec((None, tiles, dil, rows, width),
                                           lambda b, i: (b, i, 0, 0, 0))
    o, stats = pl.pallas_call(
        functools.partial(_band_attn_kernel, dil=dil),
        name=f"band_attn_d{dil}",
        grid=(bsz, n_steps),
        in_specs=[q_spec, *k_specs, *v_specs, _resident(bias.shape), _resident((2, 1, LANES))],
        out_specs=[out_block(gw), out_block(LANES)],
        out_shape=[jax.ShapeDtypeStruct((bsz, n_tiles, dil, rows, gw), BF16),
                   jax.ShapeDtypeStruct((bsz, n_tiles, dil, rows, LANES), F32)],
        compiler_params=_params(2),
    )(view, view, view, view, view, view, view, bias, headmask)
    return o.reshape(bsz, s, gw), stats.reshape(bsz, s, LANES)


def _attn_out_kernel(h_ref, o0_ref, o1_ref, o2_ref, s0_ref, s1_ref, s2_ref, z_ref, expand_ref,
                     wo_ref, out_ref, slab_ref):
    gw, tm = GROUP_WIDTH, TILE
    n_heads = HEADS_PER_GROUP

    def to_token_order(x, dil):
        if dil == 1:
            return x
        rows = tm // dil
        n = x.shape[1] // LANES
        for c in range(n):
            for r in range(dil):
                slab_ref[c, pl.ds(r, rows, stride=dil), :] = (
                    x[r * rows:(r + 1) * rows, c * LANES:(c + 1) * LANES])
        return jnp.concatenate([slab_ref[c] for c in range(n)], axis=1)

    dils = [dil for _, dil in DIL_PAIRS]
    ms = [to_token_order(ref[0], dil) for ref, dil in zip((s0_ref, s1_ref, s2_ref), dils)]
    ls = [pltpu.roll(m, LANES - n_heads, axis=1) for m in ms]
    top = jnp.maximum(jnp.maximum(ms[0], ms[1]), ms[2])
    ws = [jnp.exp2(m - top) for m in ms]
    denom = ls[0] * ws[0] + ls[1] * ws[1] + ls[2] * ws[2]
    lane = lax.broadcasted_iota(jnp.int32, (tm, LANES), 1)
    denom = jnp.where(lane < n_heads, denom, 1.0)
    acc = h_ref[0]
    for g, o_ref in enumerate((o0_ref, o1_ref, o2_ref)):
        scale = ws[g] / denom
        hi = scale.astype(BF16)
        lo = (scale - hi.astype(F32)).astype(BF16)
        spread = jnp.dot(jnp.concatenate([hi, lo], axis=1), expand_ref[...],
                         preferred_element_type=F32)
        z = z_ref[0, :, g * gw:(g + 1) * gw].astype(F32)
        y = to_token_order(o_ref[0].astype(F32), dils[g]) * spread * _silu(z)
        acc = acc + jnp.dot(y.astype(BF16), wo_ref[g], preferred_element_type=F32)
    out_ref[0] = acc


def _attn_out(h, outs, stats, z, w_out):
    bsz, s, d = h.shape
    gw, tm = GROUP_WIDTH, TILE
    assert s % tm == 0 and w_out.shape == (N_GROUPS * gw, d)
    expand = jnp.asarray(
        np.arange(2 * LANES)[:, None] % LANES == np.arange(gw)[None, :] // HEAD_DIM, BF16)
    tok = lambda width: pl.BlockSpec((1, tm, width), lambda b, i: (b, i, 0))
    return pl.pallas_call(
        _attn_out_kernel,
        name="attn_out",
        grid=(bsz, s // tm),
        in_specs=[tok(d)] + [tok(gw)] * 3 + [tok(LANES)] * 3 + [
            tok(N_GROUPS * gw), _resident((2 * LANES, gw)), _resident((N_GROUPS, gw, d))],
        out_specs=tok(d),
        out_shape=jax.ShapeDtypeStruct(h.shape, h.dtype),
        scratch_shapes=[pltpu.VMEM((gw // LANES, tm, LANES), F32)],
        compiler_params=_params(2),
    )(h, *outs, *stats, z, expand, w_out.astype(BF16).reshape(N_GROUPS, gw, d))


def _attn_layer(h, g, w_in, q_gain, k_gain, rel_table, w_out):
    qkv, z = _attn_proj(h, g, w_in, q_gain, k_gain)
    outs, stats = [], []
    for grp, (window, dil) in enumerate(DIL_PAIRS):
        assert (window // 2) // dil == HALF_WINDOW
        o, st = _band_attention(qkv, grp, dil, _band_bias(rel_table, grp, dil))
        outs.append(o)
        stats.append(st)
    return _attn_out(h, outs, stats, z, w_out)


def kernel(x, norm_g, conv_w_in, conv_kernel, conv_bias, conv_w_out, attn_w_in, q_norm_g,
           k_norm_g, attn_w_out, rel_bias_table):
    h = x
    for layer in range(norm_g.shape[0]):
        j = layer // 2
        if layer % 2 == 0:
            h = _conv_layer(h, norm_g[layer], conv_w_in[j], conv_kernel[j], conv_bias[j],
                            conv_w_out[j])
        else:
            h = _attn_layer(h, norm_g[layer], attn_w_in[j], q_norm_g[j], k_norm_g[j],
                            rel_bias_table, attn_w_out[j])
    return h
```

```python
import functools

import jax
import jax.numpy as jnp
import numpy as np
from jax import lax
from jax.experimental import pallas as pl
from jax.experimental.pallas import tpu as pltpu

EPS = 1e-6
HEAD_DIM = 64
HEADS_PER_GROUP = 8
GROUP_WIDTH = HEAD_DIM * HEADS_PER_GROUP
DIL_PAIRS = ((128, 1), (512, 4), (2048, 16))
N_GROUPS = len(DIL_PAIRS)
HALF_WINDOW = 64
REL_BUCKETS = 32
REL_MAX_DIST = 1024
MASKED = -1e30
LOG2_E = 1.4426950408889634

LANES = 128
MXU_DIM = 256
Q_BLOCK = 128
K_WINDOW = Q_BLOCK + 2 * HALF_WINDOW
TILE = 512
ATTN_TILES = 4
VMEM_LIMIT_BYTES = 56 * 1024 * 1024

BF16 = jnp.bfloat16
F32 = jnp.float32


def _silu(z):
    half = 0.5 * z
    return half + half * jnp.tanh(half)


def _rmsnorm(x, g):
    ms = jnp.mean(x * x, axis=-1, keepdims=True)
    return x * lax.rsqrt(ms + EPS) * g


def _resident(shape):
    zeros = (0,) * len(shape)
    return pl.BlockSpec(shape, lambda *_: zeros, pipeline_mode=pl.Buffered(1))


def _params(n_grid_axes):
    return pltpu.CompilerParams(dimension_semantics=("arbitrary",) * n_grid_axes,
                                vmem_limit_bytes=VMEM_LIMIT_BYTES)


CONV_TOKENS = 1024
CONV_HALO = 8
CONV_CHUNK = 512


def _conv_layer_kernel(xp_ref, x_ref, xn_ref, g_ref, w_ref, cw_ref, cb_ref, wo_ref, o_ref):
    i = pl.program_id(1)
    last = pl.num_programs(1) - 1
    tm, halo, ce = CONV_TOKENS, CONV_HALO, CONV_CHUNK
    e = wo_ref.shape[0]
    g = g_ref[...]
    x = x_ref[0]
    xp = jnp.where(i > 0, xp_ref[0], 0.0)
    xn = jnp.where(i < last, xn_ref[0], 0.0)
    hn32 = _rmsnorm(x, g)
    hn = hn32.astype(BF16)
    hne = jnp.concatenate([_rmsnorm(xp, g), hn32, _rmsnorm(xn, g)], axis=0).astype(BF16)
    o_ref[0] = x

    for j in range(e // ce):
        ch = slice(j * ce, (j + 1) * ce)

        def proj(lhs, part):
            return jnp.dot(lhs, w_ref[:, part * e + j * ce:part * e + (j + 1) * ce],
                           preferred_element_type=F32)

        p = proj(hne, 1) * proj(hne, 2)
        conv = (cw_ref[0:1, ch] * p[halo - 1:halo - 1 + tm] + cw_ref[1:2, ch] * p[halo:halo + tm]
                + cw_ref[2:3, ch] * p[halo + 1:halo + 1 + tm] + cb_ref[:, ch])
        y = proj(hn, 0) * conv * _silu(proj(hn, 3))
        o_ref[0] += jnp.dot(y.astype(BF16), wo_ref[ch, :], preferred_element_type=F32)


def _conv_layer(x, g, w_in, conv_w, conv_b, w_out):
    bsz, s, d = x.shape
    e = w_out.shape[0]
    tm, halo, ce = CONV_TOKENS, CONV_HALO, CONV_CHUNK
    assert s % tm == 0 and tm % halo == 0 and e % ce == 0 and w_in.shape == (d, 4 * e)
    per_tile = tm // halo
    n_halo_blocks = s // halo
    return pl.pallas_call(
        _conv_layer_kernel,
        name="conv_layer",
        grid=(bsz, s // tm),
        in_specs=[
            pl.BlockSpec((1, halo, d), lambda b, i: (b, jnp.maximum(i * per_tile - 1, 0), 0)),
            pl.BlockSpec((1, tm, d), lambda b, i: (b, i, 0)),
            pl.BlockSpec((1, halo, d),
                         lambda b, i: (b, jnp.minimum((i + 1) * per_tile, n_halo_blocks - 1), 0)),
            _resident((1, d)),
            _resident((d, 4 * e)),
            _resident((3, e)),
            _resident((1, e)),
            _resident((e, d)),
        ],
        out_specs=pl.BlockSpec((1, tm, d), lambda b, i: (b, i, 0)),
        out_shape=jax.ShapeDtypeStruct(x.shape, x.dtype),
        compiler_params=_params(2),
    )(x, x, x, g.reshape(1, d), w_in.astype(BF16), conv_w, conv_b.reshape(1, e),
      w_out.astype(BF16))


def _attn_proj_kernel(h_ref, g_ref, w_ref, gain_ref, headmean_ref, qkv_ref, z_ref, slab_ref):
    gw, tm = GROUP_WIDTH, TILE
    n_slabs = slab_ref.shape[0]
    n_qkv = 3 * N_GROUPS
    hn32 = _rmsnorm(h_ref[0], g_ref[...])
    hn = hn32.astype(BF16)
    for c in range(n_slabs):
        slab_ref[c] = hn32[:, c * LANES:(c + 1) * LANES]
    for grp, (_, dil) in enumerate(DIL_PAIRS):
        if dil == 1:
            hg = hn
        else:
            hg = jnp.concatenate(
                [jnp.concatenate([slab_ref[c, pl.ds(r, tm // dil, stride=dil), :]
                                  for r in range(dil)], axis=0) for c in range(n_slabs)],
                axis=1).astype(BF16)
        for t in range(3):
            c = 3 * grp + t
            y = jnp.dot(hg, w_ref[:, c * gw:(c + 1) * gw], preferred_element_type=F32)
            if t < 2:
                sq = (y * y).astype(BF16)
                ms = jnp.concatenate(
                    [jnp.dot(sq[:, k:k + MXU_DIM], headmean_ref[...], preferred_element_type=F32)
                     for k in range(0, gw, MXU_DIM)], axis=1)
                y = y * lax.rsqrt(ms + EPS) * gain_ref[c]
            qkv_ref[c, 0] = y.astype(BF16)
    for c in range(N_GROUPS):
        z = jnp.dot(hn, w_ref[:, (n_qkv + c) * gw:(n_qkv + c + 1) * gw],
                    preferred_element_type=F32)
        z_ref[0, :, c * gw:(c + 1) * gw] = z.astype(BF16)


def _attn_proj(h, g, w_in, q_gain, k_gain):
    bsz, s, d = h.shape
    gw, tm = GROUP_WIDTH, TILE
    n_qkv = 3 * N_GROUPS
    assert s % tm == 0 and d % LANES == 0 and w_in.shape == (d, (n_qkv + N_GROUPS) * gw)
    gains = jnp.stack([q_gain.reshape(N_GROUPS, gw) * (HEAD_DIM ** -0.5 * LOG2_E),
                       k_gain.reshape(N_GROUPS, gw),
                       jnp.ones((N_GROUPS, gw), F32)], axis=1).reshape(n_qkv, 1, gw)
    head = np.arange(MXU_DIM) // HEAD_DIM
    headmean = jnp.asarray((head[:, None] == head[None, :]) / HEAD_DIM, BF16)
    return pl.pallas_call(
        _attn_proj_kernel,
        name="attn_proj",
        grid=(bsz, s // tm),
        in_specs=[
            pl.BlockSpec((1, tm, d), lambda b, i: (b, i, 0)),
            _resident((1, d)),
            _resident(w_in.shape),
            _resident((n_qkv, 1, gw)),
            _resident((MXU_DIM, MXU_DIM)),
        ],
        out_specs=[
            pl.BlockSpec((n_qkv, 1, tm, gw), lambda b, i: (0, b, i, 0)),
            pl.BlockSpec((1, tm, N_GROUPS * gw), lambda b, i: (b, i, 0)),
        ],
        out_shape=[
            jax.ShapeDtypeStruct((n_qkv, bsz, s, gw), BF16),
            jax.ShapeDtypeStruct((bsz, s, N_GROUPS * gw), BF16),
        ],
        scratch_shapes=[pltpu.VMEM((d // LANES, tm, LANES), F32)],
        compiler_params=_params(2),
    )(h, g.reshape(1, d), w_in.astype(BF16), gains, headmean)


def _class_row_spans(prev_ref, cur_ref, next_ref, start, stop):
    rows = cur_ref.shape[2]
    seg = cur_ref.shape[0] * rows
    spans, pos = [], start
    while pos < stop:
        if pos < 0:
            ref, base, limit = prev_ref, pos + HALF_WINDOW, 0
        elif pos >= seg:
            ref, base, limit = next_ref, pos - seg, stop
        else:
            ref, base, limit = cur_ref, pos, seg
        tile, first = divmod(base, ref.shape[2])
        n = min(min(stop, limit) - pos, ref.shape[2] - first)
        spans.append((ref, tile, first, n))
        pos += n
    return spans


def _band_attn_kernel(q_ref, kp_ref, kc_ref, kn_ref, vp_ref, vc_ref, vn_ref, bias_ref,
                      headmask_ref, o_ref, stat_ref, *, dil):
    hw, qb = HALF_WINDOW, Q_BLOCK
    n_heads = HEADS_PER_GROUP
    blocks_per_class = q_ref.shape[0] * q_ref.shape[2] // qb
    step, last_step = pl.program_id(1), pl.num_programs(1) - 1
    lane = lax.broadcasted_iota(jnp.int32, (qb, LANES), 1)
    low_half = lane < HEAD_DIM

    def gather(spans, r, cols):
        parts = [ref[tile, r, first:first + n, cols] for ref, tile, first, n in spans]
        return parts[0] if len(parts) == 1 else jnp.concatenate(parts, axis=0)

    def scatter(ref, spans, r, cols, value):
        done = 0
        for _, tile, first, n in spans:
            ref[tile, r, first:first + n, cols] = value[done:done + n]
            done += n

    for r in range(dil):
        for u in range(blocks_per_class):
            q_spans = _class_row_spans(None, q_ref, None, u * qb, (u + 1) * qb)
            k_spans = _class_row_spans(kp_ref, kc_ref, kn_ref, u * qb - hw, (u + 1) * qb + hw)
            v_spans = _class_row_spans(vp_ref, vc_ref, vn_ref, u * qb - hw, (u + 1) * qb + hw)
            variant = 1
            if u == 0:
                variant = jnp.where(step == 0, 0, variant)
            if u == blocks_per_class - 1:
                variant = jnp.where(step == last_step, 2, variant)
            stats = jnp.zeros((qb, LANES), F32)
            for pair in range(n_heads // 2):
                cols = slice(pair * LANES, (pair + 1) * LANES)
                q2 = gather(q_spans, r, cols)
                qq = jnp.concatenate([q2 * headmask_ref[0], q2 * headmask_ref[1]], axis=0)
                kw = gather(k_spans, r, cols)
                vw = jnp.concatenate([gather(v_spans, r, cols),
                                      jnp.ones((K_WINDOW, LANES), BF16)], axis=1)
                s = lax.dot_general(qq, kw, (((1,), (1,)), ((), ())),
                                    preferred_element_type=F32)
                s = s + bias_ref[variant, pair]
                m = jnp.max(s, axis=-1, keepdims=True)
                e = jnp.exp2(s - m)
                o2 = jnp.dot(e.astype(BF16), vw, preferred_element_type=F32)
                l = o2[:, LANES:]
                o = jnp.where(low_half, o2[:qb, :LANES], o2[qb:, :LANES]).astype(BF16)
                scatter(o_ref, q_spans, r, cols, o)
                for k in range(2):
                    head = 2 * pair + k
                    stats = jnp.where(lane == head, m[k * qb:(k + 1) * qb], stats)
                    stats = jnp.where(lane == n_heads + head, l[k * qb:(k + 1) * qb], stats)
            scatter(stat_ref, q_spans, r, slice(None), stats)


def _t5_bucket(rel):
    nb = REL_BUCKETS // 2
    ret = (rel > 0).astype(np.int32) * nb
    n = np.abs(rel)
    max_exact = nb // 2
    large = max_exact + (np.log(np.maximum(n, 1) / max_exact)
                         / np.log(REL_MAX_DIST / max_exact) * (nb - max_exact)).astype(np.int32)
    large = np.minimum(large, nb - 1)
    return ret + np.where(n < max_exact, n, large).astype(np.int32)


def _band_bias(rel_table, group, dil):
    hw, qb, kw = HALF_WINDOW, Q_BLOCK, K_WINDOW
    n_heads = HEADS_PER_GROUP
    buckets = _t5_bucket(np.arange(-hw, hw + 1) * dil)
    heads = slice(group * n_heads, (group + 1) * n_heads)
    per_offset = rel_table[buckets][:, heads].astype(F32).T * LOG2_E
    width = qb + kw
    pad = jnp.full((n_heads, qb - 1), MASKED, F32)
    vec = jnp.concatenate([pad, per_offset, pad, jnp.full((n_heads, 2), MASKED, F32)], axis=1)
    assert vec.shape[1] == width + 1
    skew = jnp.tile(vec, (1, qb))[:, :qb * width].reshape(n_heads, qb, width)
    mid = skew[:, :, qb - 1:qb - 1 + kw]
    kj = np.arange(kw)[None, None, :]
    first = jnp.where(kj >= hw, mid, MASKED)
    final = jnp.where(kj < hw + qb, mid, MASKED)
    return jnp.stack([first, mid, final]).reshape(3, n_heads // 2, 2 * qb, kw)


def _band_attention(qkv, group, dil, bias):
    n_qkv, bsz, s, gw = qkv.shape
    hw, qb, tiles = HALF_WINDOW, Q_BLOCK, ATTN_TILES
    rows = TILE // dil
    n_tiles = s // TILE
    n_steps = n_tiles // tiles
    assert s % (TILE * tiles) == 0 and (tiles * rows) % qb == 0 and n_steps >= 2
    assert rows % hw == 0 or hw % rows == 0
    view = qkv.reshape(n_qkv, bsz, n_tiles, dil, rows, gw)
    headmask = jnp.asarray(
        (np.arange(LANES)[None, :] // HEAD_DIM == np.arange(2)[:, None])[:, None, :], BF16)
    if rows >= hw:
        halo_block = (None, None, 1, dil, hw, gw)
        sub = rows // hw
        prev_idx = lambda i: (jnp.maximum(tiles * i - 1, 0), 0, sub - 1, 0)
        next_idx = lambda i: (jnp.minimum(tiles * (i + 1), n_tiles - 1), 0, 0, 0)
    else:
        per = hw // rows
        halo_block = (None, None, per, dil, rows, gw)
        prev_idx = lambda i: (jnp.maximum(tiles // per * i - 1, 0), 0, 0, 0)
        next_idx = lambda i: (jnp.minimum(tiles // per * (i + 1), n_tiles // per - 1), 0, 0, 0)

    def specs(c):
        cur = pl.BlockSpec((None, None, tiles, dil, rows, gw), lambda b, i: (c, b, i, 0, 0, 0))
        prev = pl.BlockSpec(halo_block, lambda b, i: (c, b) + prev_idx(i))
        nxt = pl.BlockSpec(halo_block, lambda b, i: (c, b) + next_idx(i))
        return prev, cur, nxt

    (_, q_spec, _), k_specs, v_specs = specs(3 * group), specs(3 * group + 1), specs(3 * group + 2)
    out_block = lambda width: pl.BlockSpec((None, tiles, dil, rows, width),
                                           lambda b, i: (b, i, 0, 0, 0))
    o, stats = pl.pallas_call(
        functools.partial(_band_attn_kernel, dil=dil),
        name=f"band_attn_d{dil}",
        grid=(bsz, n_steps),
        in_specs=[q_spec, *k_specs, *v_specs, _resident(bias.shape), _resident((2, 1, LANES))],
        out_specs=[out_block(gw), out_block(LANES)],
        out_shape=[jax.ShapeDtypeStruct((bsz, n_tiles, dil, rows, gw), BF16),
                   jax.ShapeDtypeStruct((bsz, n_tiles, dil, rows, LANES), F32)],
        compiler_params=_params(2),
    )(view, view, view, view, view, view, view, bias, headmask)
    return o.reshape(bsz, s, gw), stats.reshape(bsz, s, LANES)


OUT_TOKENS = 2 * TILE


def _attn_out_kernel(h_ref, o0_ref, o1_ref, o2_ref, s0_ref, s1_ref, s2_ref, z_ref, expand_ref,
                     wo_ref, out_ref, slab_ref):
    gw, tm = GROUP_WIDTH, OUT_TOKENS
    n_heads = HEADS_PER_GROUP

    def to_token_order(x, dil):
        if dil == 1:
            return x
        rows = TILE // dil
        n = x.shape[1] // LANES
        for c in range(n):
            for base in range(0, tm, TILE):
                for r in range(dil):
                    slab_ref[c, pl.ds(base + r, rows, stride=dil), :] = (
                        x[base + r * rows:base + (r + 1) * rows, c * LANES:(c + 1) * LANES])
        return jnp.concatenate([slab_ref[c] for c in range(n)], axis=1)

    dils = [dil for _, dil in DIL_PAIRS]
    ms = [to_token_order(ref[0], dil) for ref, dil in zip((s0_ref, s1_ref, s2_ref), dils)]
    ls = [pltpu.roll(m, LANES - n_heads, axis=1) for m in ms]
    top = jnp.maximum(jnp.maximum(ms[0], ms[1]), ms[2])
    ws = [jnp.exp2(m - top) for m in ms]
    denom = ls[0] * ws[0] + ls[1] * ws[1] + ls[2] * ws[2]
    lane = lax.broadcasted_iota(jnp.int32, (tm, LANES), 1)
    denom = jnp.where(lane < n_heads, denom, 1.0)
    acc = h_ref[0]
    for g, o_ref in enumerate((o0_ref, o1_ref, o2_ref)):
        scale = ws[g] / denom
        hi = scale.astype(BF16)
        lo = (scale - hi.astype(F32)).astype(BF16)
        spread = jnp.dot(jnp.concatenate([hi, lo], axis=1), expand_ref[...],
                         preferred_element_type=F32)
        z = z_ref[0, :, g * gw:(g + 1) * gw].astype(F32)
        y = to_token_order(o_ref[0].astype(F32), dils[g]) * spread * _silu(z)
        acc = acc + jnp.dot(y.astype(BF16), wo_ref[g], preferred_element_type=F32)
    out_ref[0] = acc


def _attn_out(h, outs, stats, z, w_out):
    bsz, s, d = h.shape
    gw, tm = GROUP_WIDTH, OUT_TOKENS
    assert s % tm == 0 and tm % TILE == 0 and w_out.shape == (N_GROUPS * gw, d)
    expand = jnp.asarray(
        np.arange(2 * LANES)[:, None] % LANES == np.arange(gw)[None, :] // HEAD_DIM, BF16)
    tok = lambda width: pl.BlockSpec((1, tm, width), lambda b, i: (b, i, 0))
    return pl.pallas_call(
        _attn_out_kernel,
        name="attn_out",
        grid=(bsz, s // tm),
        in_specs=[tok(d)] + [tok(gw)] * 3 + [tok(LANES)] * 3 + [
            tok(N_GROUPS * gw), _resident((2 * LANES, gw)), _resident((N_GROUPS, gw, d))],
        out_specs=tok(d),
        out_shape=jax.ShapeDtypeStruct(h.shape, h.dtype),
        scratch_shapes=[pltpu.VMEM((gw // LANES, tm, LANES), F32)],
        compiler_params=_params(2),
    )(h, *outs, *stats, z, expand, w_out.astype(BF16).reshape(N_GROUPS, gw, d))


def _attn_layer(h, g, w_in, q_gain, k_gain, rel_table, w_out):
    qkv, z = _attn_proj(h, g, w_in, q_gain, k_gain)
    outs, stats = [], []
    for grp, (window, dil) in enumerate(DIL_PAIRS):
        assert (window // 2) // dil == HALF_WINDOW
        o, st = _band_attention(qkv, grp, dil, _band_bias(rel_table, grp, dil))
        outs.append(o)
        stats.append(st)
    return _attn_out(h, outs, stats, z, w_out)


def kernel(x, norm_g, conv_w_in, conv_kernel, conv_bias, conv_w_out, attn_w_in, q_norm_g,
           k_norm_g, attn_w_out, rel_bias_table):
    h = x
    for layer in range(norm_g.shape[0]):
        j = layer // 2
        if layer % 2 == 0:
            h = _conv_layer(h, norm_g[layer], conv_w_in[j], conv_kernel[j], conv_bias[j],
                            conv_w_out[j])
        else:
            h = _attn_layer(h, norm_g[layer], attn_w_in[j], q_norm_g[j], k_norm_g[j],
                            rel_bias_table, attn_w_out[j])
    return h
```

```python
import functools

import jax
import jax.numpy as jnp
import numpy as np
from jax import lax
from jax.experimental import pallas as pl
from jax.experimental.pallas import tpu as pltpu

EPS = 1e-6
HEAD_DIM = 64
HEADS_PER_GROUP = 8
GROUP_WIDTH = HEAD_DIM * HEADS_PER_GROUP
DIL_PAIRS = ((128, 1), (512, 4), (2048, 16))
N_GROUPS = len(DIL_PAIRS)
HALF_WINDOW = 64
REL_BUCKETS = 32
REL_MAX_DIST = 1024
MASKED = -1e30
LOG2_E = 1.4426950408889634

LANES = 128
MXU_DIM = 256
Q_BLOCK = 128
K_WINDOW = Q_BLOCK + 2 * HALF_WINDOW
TILE = 512
ATTN_TILES = 4
VMEM_LIMIT_BYTES = 56 * 1024 * 1024

BF16 = jnp.bfloat16
F32 = jnp.float32


def _silu(z):
    half = 0.5 * z
    return half + half * jnp.tanh(half)


def _rmsnorm(x, g):
    ms = jnp.mean(x * x, axis=-1, keepdims=True)
    return x * lax.rsqrt(ms + EPS) * g


def _resident(shape):
    zeros = (0,) * len(shape)
    return pl.BlockSpec(shape, lambda *_: zeros, pipeline_mode=pl.Buffered(1))


def _params(n_grid_axes):
    return pltpu.CompilerParams(dimension_semantics=("arbitrary",) * n_grid_axes,
                                vmem_limit_bytes=VMEM_LIMIT_BYTES)


CONV_TOKENS = 1024
CONV_HALO = 8
CONV_CHUNK = 512


def _conv_layer_kernel(xp_ref, x_ref, xn_ref, g_ref, w_ref, cw_ref, cb_ref, wo_ref, o_ref,
                       shift_ref):
    i = pl.program_id(1)
    last = pl.num_programs(1) - 1
    tm, halo, ce = CONV_TOKENS, CONV_HALO, CONV_CHUNK
    e = wo_ref.shape[0]
    g = g_ref[...]
    x = x_ref[0]
    xp = jnp.where(i > 0, xp_ref[0], 0.0)
    xn = jnp.where(i < last, xn_ref[0], 0.0)
    hn32 = _rmsnorm(x, g)
    hn = hn32.astype(BF16)
    hne = jnp.concatenate([_rmsnorm(xp, g), hn32, _rmsnorm(xn, g)], axis=0).astype(BF16)
    o_ref[0] = x

    for j in range(e // ce):
        ch = slice(j * ce, (j + 1) * ce)

        def proj(lhs, part):
            return jnp.dot(lhs, w_ref[:, part * e + j * ce:part * e + (j + 1) * ce],
                           preferred_element_type=F32)

        p = proj(hne, 1) * proj(hne, 2)
        slabs = shift_ref.at[j % 2]
        for c in range(ce // LANES):
            slabs[c] = p[:, c * LANES:(c + 1) * LANES]

        def shifted(by):
            return jnp.concatenate([slabs[c, halo + by:halo + by + tm, :]
                                    for c in range(ce // LANES)], axis=1)

        conv = (cw_ref[0:1, ch] * shifted(-1) + cw_ref[1:2, ch] * p[halo:halo + tm]
                + cw_ref[2:3, ch] * shifted(1) + cb_ref[:, ch])
        y = proj(hn, 0) * conv * _silu(proj(hn, 3))
        o_ref[0] += jnp.dot(y.astype(BF16), wo_ref[ch, :], preferred_element_type=F32)


def _conv_layer(x, g, w_in, conv_w, conv_b, w_out):
    bsz, s, d = x.shape
    e = w_out.shape[0]
    tm, halo, ce = CONV_TOKENS, CONV_HALO, CONV_CHUNK
    assert s % tm == 0 and tm % halo == 0 and e % ce == 0 and w_in.shape == (d, 4 * e)
    per_tile = tm // halo
    n_halo_blocks = s // halo
    return pl.pallas_call(
        _conv_layer_kernel,
        name="conv_layer",
        grid=(bsz, s // tm),
        in_specs=[
            pl.BlockSpec((1, halo, d), lambda b, i: (b, jnp.maximum(i * per_tile - 1, 0), 0)),
            pl.BlockSpec((1, tm, d), lambda b, i: (b, i, 0)),
            pl.BlockSpec((1, halo, d),
                         lambda b, i: (b, jnp.minimum((i + 1) * per_tile, n_halo_blocks - 1), 0)),
            _resident((1, d)),
            _resident((d, 4 * e)),
            _resident((3, e)),
            _resident((1, e)),
            _resident((e, d)),
        ],
        out_specs=pl.BlockSpec((1, tm, d), lambda b, i: (b, i, 0)),
        out_shape=jax.ShapeDtypeStruct(x.shape, x.dtype),
        scratch_shapes=[pltpu.VMEM((2, ce // LANES, tm + 2 * halo, LANES), F32)],
        compiler_params=_params(2),
    )(x, x, x, g.reshape(1, d), w_in.astype(BF16), conv_w, conv_b.reshape(1, e),
      w_out.astype(BF16))


def _attn_proj_kernel(h_ref, g_ref, w_ref, gain_ref, headmean_ref, qkv_ref, z_ref, slab_ref):
    gw, tm = GROUP_WIDTH, TILE
    n_slabs = slab_ref.shape[0]
    n_qkv = 3 * N_GROUPS
    hn32 = _rmsnorm(h_ref[0], g_ref[...])
    hn = hn32.astype(BF16)
    for c in range(n_slabs):
        slab_ref[c] = hn32[:, c * LANES:(c + 1) * LANES]
    for grp, (_, dil) in enumerate(DIL_PAIRS):
        if dil == 1:
            hg = hn
        else:
            hg = jnp.concatenate(
                [jnp.concatenate([slab_ref[c, pl.ds(r, tm // dil, stride=dil), :]
                                  for r in range(dil)], axis=0) for c in range(n_slabs)],
                axis=1).astype(BF16)
        for t in range(3):
            c = 3 * grp + t
            y = jnp.dot(hg, w_ref[:, c * gw:(c + 1) * gw], preferred_element_type=F32)
            if t < 2:
                sq = (y * y).astype(BF16)
                ms = jnp.concatenate(
                    [jnp.dot(sq[:, k:k + MXU_DIM], headmean_ref[...], preferred_element_type=F32)
                     for k in range(0, gw, MXU_DIM)], axis=1)
                y = y * lax.rsqrt(ms + EPS) * gain_ref[c]
            qkv_ref[c, 0] = y.astype(BF16)
    for c in range(N_GROUPS):
        z = jnp.dot(hn, w_ref[:, (n_qkv + c) * gw:(n_qkv + c + 1) * gw],
                    preferred_element_type=F32)
        z_ref[0, :, c * gw:(c + 1) * gw] = z.astype(BF16)


def _attn_proj(h, g, w_in, q_gain, k_gain):
    bsz, s, d = h.shape
    gw, tm = GROUP_WIDTH, TILE
    n_qkv = 3 * N_GROUPS
    assert s % tm == 0 and d % LANES == 0 and w_in.shape == (d, (n_qkv + N_GROUPS) * gw)
    gains = jnp.stack([q_gain.reshape(N_GROUPS, gw) * (HEAD_DIM ** -0.5 * LOG2_E),
                       k_gain.reshape(N_GROUPS, gw),
                       jnp.ones((N_GROUPS, gw), F32)], axis=1).reshape(n_qkv, 1, gw)
    head = np.arange(MXU_DIM) // HEAD_DIM
    headmean = jnp.asarray((head[:, None] == head[None, :]) / HEAD_DIM, BF16)
    return pl.pallas_call(
        _attn_proj_kernel,
        name="attn_proj",
        grid=(bsz, s // tm),
        in_specs=[
            pl.BlockSpec((1, tm, d), lambda b, i: (b, i, 0)),
            _resident((1, d)),
            _resident(w_in.shape),
            _resident((n_qkv, 1, gw)),
            _resident((MXU_DIM, MXU_DIM)),
        ],
        out_specs=[
            pl.BlockSpec((n_qkv, 1, tm, gw), lambda b, i: (0, b, i, 0)),
            pl.BlockSpec((1, tm, N_GROUPS * gw), lambda b, i: (b, i, 0)),
        ],
        out_shape=[
            jax.ShapeDtypeStruct((n_qkv, bsz, s, gw), BF16),
            jax.ShapeDtypeStruct((bsz, s, N_GROUPS * gw), BF16),
        ],
        scratch_shapes=[pltpu.VMEM((d // LANES, tm, LANES), F32)],
        compiler_params=_params(2),
    )(h, g.reshape(1, d), w_in.astype(BF16), gains, headmean)


def _class_row_spans(prev_ref, cur_ref, next_ref, start, stop):
    rows = cur_ref.shape[2]
    seg = cur_ref.shape[0] * rows
    spans, pos = [], start
    while pos < stop:
        if pos < 0:
            ref, base, limit = prev_ref, pos + HALF_WINDOW, 0
        elif pos >= seg:
            ref, base, limit = next_ref, pos - seg, stop
        else:
            ref, base, limit = cur_ref, pos, seg
        tile, first = divmod(base, ref.shape[2])
        n = min(min(stop, limit) - pos, ref.shape[2] - first)
        spans.append((ref, tile, first, n))
        pos += n
    return spans


def _band_attn_kernel(q_ref, kp_ref, kc_ref, kn_ref, vp_ref, vc_ref, vn_ref, bias_ref,
                      headmask_ref, o_ref, stat_ref, *, dil):
    hw, qb = HALF_WINDOW, Q_BLOCK
    n_heads = HEADS_PER_GROUP
    blocks_per_class = q_ref.shape[0] * q_ref.shape[2] // qb
    step, last_step = pl.program_id(1), pl.num_programs(1) - 1
    lane = lax.broadcasted_iota(jnp.int32, (qb, LANES), 1)
    low_half = lane < HEAD_DIM

    def gather(spans, r, cols):
        parts = [ref[tile, r, first:first + n, cols] for ref, tile, first, n in spans]
        return parts[0] if len(parts) == 1 else jnp.concatenate(parts, axis=0)

    def scatter(ref, spans, r, cols, value):
        done = 0
        for _, tile, first, n in spans:
            ref[tile, r, first:first + n, cols] = value[done:done + n]
            done += n

    for r in range(dil):
        for u in range(blocks_per_class):
            q_spans = _class_row_spans(None, q_ref, None, u * qb, (u + 1) * qb)
            k_spans = _class_row_spans(kp_ref, kc_ref, kn_ref, u * qb - hw, (u + 1) * qb + hw)
            v_spans = _class_row_spans(vp_ref, vc_ref, vn_ref, u * qb - hw, (u + 1) * qb + hw)
            variant = 1
            if u == 0:
                variant = jnp.where(step == 0, 0, variant)
            if u == blocks_per_class - 1:
                variant = jnp.where(step == last_step, 2, variant)
            stats = jnp.zeros((qb, LANES), F32)
            for pair in range(n_heads // 2):
                cols = slice(pair * LANES, (pair + 1) * LANES)
                q2 = gather(q_spans, r, cols)
                qq = jnp.concatenate([q2 * headmask_ref[0], q2 * headmask_ref[1]], axis=0)
                kw = gather(k_spans, r, cols)
                vw = jnp.concatenate([gather(v_spans, r, cols),
                                      jnp.ones((K_WINDOW, LANES), BF16)], axis=1)
                s = lax.dot_general(qq, kw, (((1,), (1,)), ((), ())),
                                    preferred_element_type=F32)
                s = s + bias_ref[variant, pair]
                m = jnp.max(s, axis=-1, keepdims=True)
                e = jnp.exp2(s - m)
                o2 = jnp.dot(e.astype(BF16), vw, preferred_element_type=F32)
                l = o2[:, LANES:]
                o = jnp.where(low_half, o2[:qb, :LANES], o2[qb:, :LANES]).astype(BF16)
                scatter(o_ref, q_spans, r, cols, o)
                for k in range(2):
                    head = 2 * pair + k
                    stats = jnp.where(lane == head, m[k * qb:(k + 1) * qb], stats)
                    stats = jnp.where(lane == n_heads + head, l[k * qb:(k + 1) * qb], stats)
            scatter(stat_ref, q_spans, r, slice(None), stats)


def _t5_bucket(rel):
    nb = REL_BUCKETS // 2
    ret = (rel > 0).astype(np.int32) * nb
    n = np.abs(rel)
    max_exact = nb // 2
    large = max_exact + (np.log(np.maximum(n, 1) / max_exact)
                         / np.log(REL_MAX_DIST / max_exact) * (nb - max_exact)).astype(np.int32)
    large = np.minimum(large, nb - 1)
    return ret + np.where(n < max_exact, n, large).astype(np.int32)


def _band_bias(rel_table, group, dil):
    hw, qb, kw = HALF_WINDOW, Q_BLOCK, K_WINDOW
    n_heads = HEADS_PER_GROUP
    buckets = _t5_bucket(np.arange(-hw, hw + 1) * dil)
    heads = slice(group * n_heads, (group + 1) * n_heads)
    per_offset = rel_table[buckets][:, heads].astype(F32).T * LOG2_E
    width = qb + kw
    pad = jnp.full((n_heads, qb - 1), MASKED, F32)
    vec = jnp.concatenate([pad, per_offset, pad, jnp.full((n_heads, 2), MASKED, F32)], axis=1)
    assert vec.shape[1] == width + 1
    skew = jnp.tile(vec, (1, qb))[:, :qb * width].reshape(n_heads, qb, width)
    mid = skew[:, :, qb - 1:qb - 1 + kw]
    kj = np.arange(kw)[None, None, :]
    first = jnp.where(kj >= hw, mid, MASKED)
    final = jnp.where(kj < hw + qb, mid, MASKED)
    return jnp.stack([first, mid, final]).reshape(3, n_heads // 2, 2 * qb, kw)


def _band_attention(qkv, group, dil, bias):
    n_qkv, bsz, s, gw = qkv.shape
    hw, qb, tiles = HALF_WINDOW, Q_BLOCK, ATTN_TILES
    rows = TILE // dil
    n_tiles = s // TILE
    n_steps = n_tiles // tiles
    assert s % (TILE * tiles) == 0 and (tiles * rows) % qb == 0 and n_steps >= 2
    assert rows % hw == 0 or hw % rows == 0
    view = qkv.reshape(n_qkv, bsz, n_tiles, dil, rows, gw)
    headmask = jnp.asarray(
        (np.arange(LANES)[None, :] // HEAD_DIM == np.arange(2)[:, None])[:, None, :], BF16)
    if rows >= hw:
        halo_block = (None, None, 1, dil, hw, gw)
        sub = rows // hw
        prev_idx = lambda i: (jnp.maximum(tiles * i - 1, 0), 0, sub - 1, 0)
        next_idx = lambda i: (jnp.minimum(tiles * (i + 1), n_tiles - 1), 0, 0, 0)
    else:
        per = hw // rows
        halo_block = (None, None, per, dil, rows, gw)
        prev_idx = lambda i: (jnp.maximum(tiles // per * i - 1, 0), 0, 0, 0)
        next_idx = lambda i: (jnp.minimum(tiles // per * (i + 1), n_tiles // per - 1), 0, 0, 0)

    def specs(c):
        cur = pl.BlockSpec((None, None, tiles, dil, rows, gw), lambda b, i: (c, b, i, 0, 0, 0))
        prev = pl.BlockSpec(halo_block, lambda b, i: (c, b) + prev_idx(i))
        nxt = pl.BlockSpec(halo_block, lambda b, i: (c, b) + next_idx(i))
        return prev, cur, nxt

    (_, q_spec, _), k_specs, v_specs = specs(3 * group), specs(3 * group + 1), specs(3 * group + 2)
    out_block = lambda width: pl.BlockSpec((None, tiles, dil, rows, width),
                                           lambda b, i: (b, i, 0, 0, 0))
    o, stats = pl.pallas_call(
        functools.partial(_band_attn_kernel, dil=dil),
        name=f"band_attn_d{dil}",
        grid=(bsz, n_steps),
        in_specs=[q_spec, *k_specs, *v_specs, _resident(bias.shape), _resident((2, 1, LANES))],
        out_specs=[out_block(gw), out_block(LANES)],
        out_shape=[jax.ShapeDtypeStruct((bsz, n_tiles, dil, rows, gw), BF16),
                   jax.ShapeDtypeStruct((bsz, n_tiles, dil, rows, LANES), F32)],
        compiler_params=_params(2),
    )(view, view, view, view, view, view, view, bias, headmask)
    return o.reshape(bsz, s, gw), stats.reshape(bsz, s, LANES)


OUT_TOKENS = 2 * TILE


def _attn_out_kernel(h_ref, o0_ref, o1_ref, o2_ref, s0_ref, s1_ref, s2_ref, z_ref, expand_ref,
                     wo_ref, out_ref, slab_ref):
    gw, tm = GROUP_WIDTH, OUT_TOKENS
    n_heads = HEADS_PER_GROUP

    def to_token_order(x, dil):
        if dil == 1:
            return x
        rows = TILE // dil
        n = x.shape[1] // LANES
        for c in range(n):
            for base in range(0, tm, TILE):
                for r in range(dil):
                    slab_ref[c, pl.ds(base + r, rows, stride=dil), :] = (
                        x[base + r * rows:base + (r + 1) * rows, c * LANES:(c + 1) * LANES])
        return jnp.concatenate([slab_ref[c] for c in range(n)], axis=1)

    dils = [dil for _, dil in DIL_PAIRS]
    ms = [to_token_order(ref[0], dil) for ref, dil in zip((s0_ref, s1_ref, s2_ref), dils)]
    ls = [pltpu.roll(m, LANES - n_heads, axis=1) for m in ms]
    top = jnp.maximum(jnp.maximum(ms[0], ms[1]), ms[2])
    ws = [jnp.exp2(m - top) for m in ms]
    denom = ls[0] * ws[0] + ls[1] * ws[1] + ls[2] * ws[2]
    lane = lax.broadcasted_iota(jnp.int32, (tm, LANES), 1)
    denom = jnp.where(lane < n_heads, denom, 1.0)
    acc = h_ref[0]
    for g, o_ref in enumerate((o0_ref, o1_ref, o2_ref)):
        scale = ws[g] / denom
        hi = scale.astype(BF16)
        lo = (scale - hi.astype(F32)).astype(BF16)
        spread = jnp.dot(jnp.concatenate([hi, lo], axis=1), expand_ref[...],
                         preferred_element_type=F32)
        z = z_ref[0, :, g * gw:(g + 1) * gw].astype(F32)
        y = to_token_order(o_ref[0].astype(F32), dils[g]) * spread * _silu(z)
        acc = acc + jnp.dot(y.astype(BF16), wo_ref[g], preferred_element_type=F32)
    out_ref[0] = acc


def _attn_out(h, outs, stats, z, w_out):
    bsz, s, d = h.shape
    gw, tm = GROUP_WIDTH, OUT_TOKENS
    assert s % tm == 0 and tm % TILE == 0 and w_out.shape == (N_GROUPS * gw, d)
    expand = jnp.asarray(
        np.arange(2 * LANES)[:, None] % LANES == np.arange(gw)[None, :] // HEAD_DIM, BF16)
    tok = lambda width: pl.BlockSpec((1, tm, width), lambda b, i: (b, i, 0))
    return pl.pallas_call(
        _attn_out_kernel,
        name="attn_out",
        grid=(bsz, s // tm),
        in_specs=[tok(d)] + [tok(gw)] * 3 + [tok(LANES)] * 3 + [
            tok(N_GROUPS * gw), _resident((2 * LANES, gw)), _resident((N_GROUPS, gw, d))],
        out_specs=tok(d),
        out_shape=jax.ShapeDtypeStruct(h.shape, h.dtype),
        scratch_shapes=[pltpu.VMEM((gw // LANES, tm, LANES), F32)],
        compiler_params=_params(2),
    )(h, *outs, *stats, z, expand, w_out.astype(BF16).reshape(N_GROUPS, gw, d))


def _attn_layer(h, g, w_in, q_gain, k_gain, rel_table, w_out):
    qkv, z = _attn_proj(h, g, w_in, q_gain, k_gain)
    outs, stats = [], []
    for grp, (window, dil) in enumerate(DIL_PAIRS):
        assert (window // 2) // dil == HALF_WINDOW
        o, st = _band_attention(qkv, grp, dil, _band_bias(rel_table, grp, dil))
        outs.append(o)
        stats.append(st)
    return _attn_out(h, outs, stats, z, w_out)


def kernel(x, norm_g, conv_w_in, conv_kernel, conv_bias, conv_w_out, attn_w_in, q_norm_g,
           k_norm_g, attn_w_out, rel_bias_table):
    h = x
    for layer in range(norm_g.shape[0]):
        j = layer // 2
        if layer % 2 == 0:
            h = _conv_layer(h, norm_g[layer], conv_w_in[j], conv_kernel[j], conv_bias[j],
                            conv_w_out[j])
        else:
            h = _attn_layer(h, norm_g[layer], attn_w_in[j], q_norm_g[j], k_norm_g[j],
                            rel_bias_table, attn_w_out[j])
    return h
```

```python
import functools

import jax
import jax.numpy as jnp
import numpy as np
from jax import lax
from jax.experimental import pallas as pl
from jax.experimental.pallas import tpu as pltpu

EPS = 1e-6
HEAD_DIM = 64
HEADS_PER_GROUP = 8
GROUP_WIDTH = HEAD_DIM * HEADS_PER_GROUP
DIL_PAIRS = ((128, 1), (512, 4), (2048, 16))
N_GROUPS = len(DIL_PAIRS)
HALF_WINDOW = 64
REL_BUCKETS = 32
REL_MAX_DIST = 1024
MASKED = -1e30
LOG2_E = 1.4426950408889634

LANES = 128
MXU_DIM = 256
Q_BLOCK = 128
K_WINDOW = Q_BLOCK + 2 * HALF_WINDOW
TILE = 512
ATTN_TILES = 4
VMEM_LIMIT_BYTES = 56 * 1024 * 1024

BF16 = jnp.bfloat16
F32 = jnp.float32


def _silu(z):
    half = 0.5 * z
    return half + half * jnp.tanh(half)


def _rmsnorm(x, g):
    ms = jnp.mean(x * x, axis=-1, keepdims=True)
    return x * lax.rsqrt(ms + EPS) * g


def _resident(shape):
    zeros = (0,) * len(shape)
    return pl.BlockSpec(shape, lambda *_: zeros, pipeline_mode=pl.Buffered(1))


def _params(n_grid_axes):
    return pltpu.CompilerParams(dimension_semantics=("arbitrary",) * n_grid_axes,
                                vmem_limit_bytes=VMEM_LIMIT_BYTES)


CONV_TOKENS = 1024
CONV_HALO = 8
CONV_CHUNK = 512


def _conv_layer_kernel(xp_ref, x_ref, xn_ref, g_ref, w_ref, cw_ref, cb_ref, wo_ref, o_ref,
                       shift_ref):
    i = pl.program_id(1)
    last = pl.num_programs(1) - 1
    tm, halo, ce = CONV_TOKENS, CONV_HALO, CONV_CHUNK
    e = wo_ref.shape[0]
    g = g_ref[...]
    x = x_ref[0]
    xp = jnp.where(i > 0, xp_ref[0], 0.0)
    xn = jnp.where(i < last, xn_ref[0], 0.0)
    hn32 = _rmsnorm(x, g)
    hn = hn32.astype(BF16)
    hne = jnp.concatenate([_rmsnorm(xp, g), hn32, _rmsnorm(xn, g)], axis=0).astype(BF16)
    o_ref[0] = x

    for j in range(e // ce):
        ch = slice(j * ce, (j + 1) * ce)

        def proj(lhs, part):
            return jnp.dot(lhs, w_ref[:, part * e + j * ce:part * e + (j + 1) * ce],
                           preferred_element_type=F32)

        p = proj(hne, 1) * proj(hne, 2)
        slabs = shift_ref.at[j % 2]
        for c in range(ce // LANES):
            slabs[c] = p[:, c * LANES:(c + 1) * LANES]

        def shifted(by):
            return jnp.concatenate([slabs[c, halo + by:halo + by + tm, :]
                                    for c in range(ce // LANES)], axis=1)

        conv = (cw_ref[0:1, ch] * shifted(-1) + cw_ref[1:2, ch] * p[halo:halo + tm]
                + cw_ref[2:3, ch] * shifted(1) + cb_ref[:, ch])
        y = proj(hn, 0) * conv * _silu(proj(hn, 3))
        o_ref[0] += jnp.dot(y.astype(BF16), wo_ref[ch, :], preferred_element_type=F32)


def _conv_layer(x, g, w_in, conv_w, conv_b, w_out):
    bsz, s, d = x.shape
    e = w_out.shape[0]
    tm, halo, ce = CONV_TOKENS, CONV_HALO, CONV_CHUNK
    assert s % tm == 0 and tm % halo == 0 and e % ce == 0 and w_in.shape == (d, 4 * e)
    per_tile = tm // halo
    n_halo_blocks = s // halo
    return pl.pallas_call(
        _conv_layer_kernel,
        name="conv_layer",
        grid=(bsz, s // tm),
        in_specs=[
            pl.BlockSpec((1, halo, d), lambda b, i: (b, jnp.maximum(i * per_tile - 1, 0), 0)),
            pl.BlockSpec((1, tm, d), lambda b, i: (b, i, 0)),
            pl.BlockSpec((1, halo, d),
                         lambda b, i: (b, jnp.minimum((i + 1) * per_tile, n_halo_blocks - 1), 0)),
            _resident((1, d)),
            _resident((d, 4 * e)),
            _resident((3, e)),
            _resident((1, e)),
            _resident((e, d)),
        ],
        out_specs=pl.BlockSpec((1, tm, d), lambda b, i: (b, i, 0)),
        out_shape=jax.ShapeDtypeStruct(x.shape, x.dtype),
        scratch_shapes=[pltpu.VMEM((2, ce // LANES, tm + 2 * halo, LANES), F32)],
        compiler_params=_params(2),
    )(x, x, x, g.reshape(1, d), w_in.astype(BF16), conv_w, conv_b.reshape(1, e),
      w_out.astype(BF16))


def _attn_proj_kernel(h_ref, g_ref, w_ref, gain_ref, headmean_ref, qkv_ref, z_ref, slab_ref):
    gw, tm = GROUP_WIDTH, TILE
    n_slabs = slab_ref.shape[0]
    n_qkv = 3 * N_GROUPS
    hn32 = _rmsnorm(h_ref[0], g_ref[...])
    hn = hn32.astype(BF16)
    for c in range(n_slabs):
        slab_ref[c] = hn32[:, c * LANES:(c + 1) * LANES]
    for grp, (_, dil) in enumerate(DIL_PAIRS):
        if dil == 1:
            hg = hn
        else:
            hg = jnp.concatenate(
                [jnp.concatenate([slab_ref[c, pl.ds(r, tm // dil, stride=dil), :]
                                  for r in range(dil)], axis=0) for c in range(n_slabs)],
                axis=1).astype(BF16)
        for t in range(3):
            c = 3 * grp + t
            y = jnp.dot(hg, w_ref[:, c * gw:(c + 1) * gw], preferred_element_type=F32)
            if t < 2:
                sq = (y * y).astype(BF16)
                ms = jnp.concatenate(
                    [jnp.dot(sq[:, k:k + MXU_DIM], headmean_ref[...], preferred_element_type=F32)
                     for k in range(0, gw, MXU_DIM)], axis=1)
                y = y * lax.rsqrt(ms + EPS) * gain_ref[c]
            qkv_ref[c, 0] = y.astype(BF16)
    for c in range(N_GROUPS):
        z = jnp.dot(hn, w_ref[:, (n_qkv + c) * gw:(n_qkv + c + 1) * gw],
                    preferred_element_type=F32)
        z_ref[0, :, c * gw:(c + 1) * gw] = z.astype(BF16)


def _attn_proj(h, g, w_in, q_gain, k_gain):
    bsz, s, d = h.shape
    gw, tm = GROUP_WIDTH, TILE
    n_qkv = 3 * N_GROUPS
    assert s % tm == 0 and d % LANES == 0 and w_in.shape == (d, (n_qkv + N_GROUPS) * gw)
    gains = jnp.stack([q_gain.reshape(N_GROUPS, gw) * (HEAD_DIM ** -0.5 * LOG2_E),
                       k_gain.reshape(N_GROUPS, gw),
                       jnp.ones((N_GROUPS, gw), F32)], axis=1).reshape(n_qkv, 1, gw)
    head = np.arange(MXU_DIM) // HEAD_DIM
    headmean = jnp.asarray((head[:, None] == head[None, :]) / HEAD_DIM, BF16)
    return pl.pallas_call(
        _attn_proj_kernel,
        name="attn_proj",
        grid=(bsz, s // tm),
        in_specs=[
            pl.BlockSpec((1, tm, d), lambda b, i: (b, i, 0)),
            _resident((1, d)),
            _resident(w_in.shape),
            _resident((n_qkv, 1, gw)),
            _resident((MXU_DIM, MXU_DIM)),
        ],
        out_specs=[
            pl.BlockSpec((n_qkv, 1, tm, gw), lambda b, i: (0, b, i, 0)),
            pl.BlockSpec((1, tm, N_GROUPS * gw), lambda b, i: (b, i, 0)),
        ],
        out_shape=[
            jax.ShapeDtypeStruct((n_qkv, bsz, s, gw), BF16),
            jax.ShapeDtypeStruct((bsz, s, N_GROUPS * gw), BF16),
        ],
        scratch_shapes=[pltpu.VMEM((d // LANES, tm, LANES), F32)],
        compiler_params=_params(2),
    )(h, g.reshape(1, d), w_in.astype(BF16), gains, headmean)


def _class_row_spans(prev_ref, cur_ref, next_ref, start, stop):
    rows = cur_ref.shape[2]
    seg = cur_ref.shape[0] * rows
    spans, pos = [], start
    while pos < stop:
        if pos < 0:
            ref, base, limit = prev_ref, pos + HALF_WINDOW, 0
        elif pos >= seg:
            ref, base, limit = next_ref, pos - seg, stop
        else:
            ref, base, limit = cur_ref, pos, seg
        tile, first = divmod(base, ref.shape[2])
        n = min(min(stop, limit) - pos, ref.shape[2] - first)
        spans.append((ref, tile, first, n))
        pos += n
    return spans


def _band_attn_kernel(q_ref, kp_ref, kc_ref, kn_ref, vp_ref, vc_ref, vn_ref, bias_ref,
                      headmask_ref, o_ref, stat_ref, *, dil):
    hw, qb = HALF_WINDOW, Q_BLOCK
    n_heads = HEADS_PER_GROUP
    blocks_per_class = q_ref.shape[0] * q_ref.shape[2] // qb
    step, last_step = pl.program_id(1), pl.num_programs(1) - 1
    lane = lax.broadcasted_iota(jnp.int32, (qb, LANES), 1)
    low_half = lane < HEAD_DIM

    def gather(spans, r, cols):
        parts = [ref[tile, r, first:first + n, cols] for ref, tile, first, n in spans]
        return parts[0] if len(parts) == 1 else jnp.concatenate(parts, axis=0)

    def scatter(ref, spans, r, slab, value):
        done = 0
        for _, tile, first, n in spans:
            piece = value[done:done + n]
            if dil == 1:
                ref[tile, r, first:first + n, slab * LANES:(slab + 1) * LANES] = (
                    piece.astype(ref.dtype))
            else:
                ref[tile, slab, pl.ds(r + dil * first, n, stride=dil), :] = piece
            done += n

    for r in range(dil):
        for u in range(blocks_per_class):
            q_spans = _class_row_spans(None, q_ref, None, u * qb, (u + 1) * qb)
            k_spans = _class_row_spans(kp_ref, kc_ref, kn_ref, u * qb - hw, (u + 1) * qb + hw)
            v_spans = _class_row_spans(vp_ref, vc_ref, vn_ref, u * qb - hw, (u + 1) * qb + hw)
            variant = 1
            if u == 0:
                variant = jnp.where(step == 0, 0, variant)
            if u == blocks_per_class - 1:
                variant = jnp.where(step == last_step, 2, variant)
            stats = jnp.zeros((qb, LANES), F32)
            for pair in range(n_heads // 2):
                cols = slice(pair * LANES, (pair + 1) * LANES)
                q2 = gather(q_spans, r, cols)
                qq = jnp.concatenate([q2 * headmask_ref[0], q2 * headmask_ref[1]], axis=0)
                kw = gather(k_spans, r, cols)
                vw = jnp.concatenate([gather(v_spans, r, cols),
                                      jnp.ones((K_WINDOW, LANES), BF16)], axis=1)
                s = lax.dot_general(qq, kw, (((1,), (1,)), ((), ())),
                                    preferred_element_type=F32)
                s = s + bias_ref[variant, pair]
                m = jnp.max(s, axis=-1, keepdims=True)
                e = jnp.exp2(s - m)
                o2 = jnp.dot(e.astype(BF16), vw, preferred_element_type=F32)
                l = o2[:, LANES:]
                o = jnp.where(low_half, o2[:qb, :LANES], o2[qb:, :LANES])
                scatter(o_ref, q_spans, r, pair, o)
                for k in range(2):
                    head = 2 * pair + k
                    stats = jnp.where(lane == head, m[k * qb:(k + 1) * qb], stats)
                    stats = jnp.where(lane == n_heads + head, l[k * qb:(k + 1) * qb], stats)
            scatter(stat_ref, q_spans, r, 0, stats)


def _t5_bucket(rel):
    nb = REL_BUCKETS // 2
    ret = (rel > 0).astype(np.int32) * nb
    n = np.abs(rel)
    max_exact = nb // 2
    large = max_exact + (np.log(np.maximum(n, 1) / max_exact)
                         / np.log(REL_MAX_DIST / max_exact) * (nb - max_exact)).astype(np.int32)
    large = np.minimum(large, nb - 1)
    return ret + np.where(n < max_exact, n, large).astype(np.int32)


def _band_bias(rel_table, group, dil):
    hw, qb, kw = HALF_WINDOW, Q_BLOCK, K_WINDOW
    n_heads = HEADS_PER_GROUP
    buckets = _t5_bucket(np.arange(-hw, hw + 1) * dil)
    heads = slice(group * n_heads, (group + 1) * n_heads)
    per_offset = rel_table[buckets][:, heads].astype(F32).T * LOG2_E
    width = qb + kw
    pad = jnp.full((n_heads, qb - 1), MASKED, F32)
    vec = jnp.concatenate([pad, per_offset, pad, jnp.full((n_heads, 2), MASKED, F32)], axis=1)
    assert vec.shape[1] == width + 1
    skew = jnp.tile(vec, (1, qb))[:, :qb * width].reshape(n_heads, qb, width)
    mid = skew[:, :, qb - 1:qb - 1 + kw]
    kj = np.arange(kw)[None, None, :]
    first = jnp.where(kj >= hw, mid, MASKED)
    final = jnp.where(kj < hw + qb, mid, MASKED)
    return jnp.stack([first, mid, final]).reshape(3, n_heads // 2, 2 * qb, kw)


def _band_attention(qkv, group, dil, bias):
    n_qkv, bsz, s, gw = qkv.shape
    hw, qb, tiles = HALF_WINDOW, Q_BLOCK, ATTN_TILES
    rows = TILE // dil
    n_tiles = s // TILE
    n_steps = n_tiles // tiles
    assert s % (TILE * tiles) == 0 and (tiles * rows) % qb == 0 and n_steps >= 2
    assert rows % hw == 0 or hw % rows == 0
    view = qkv.reshape(n_qkv, bsz, n_tiles, dil, rows, gw)
    headmask = jnp.asarray(
        (np.arange(LANES)[None, :] // HEAD_DIM == np.arange(2)[:, None])[:, None, :], BF16)
    if rows >= hw:
        halo_block = (None, None, 1, dil, hw, gw)
        sub = rows // hw
        prev_idx = lambda i: (jnp.maximum(tiles * i - 1, 0), 0, sub - 1, 0)
        next_idx = lambda i: (jnp.minimum(tiles * (i + 1), n_tiles - 1), 0, 0, 0)
    else:
        per = hw // rows
        halo_block = (None, None, per, dil, rows, gw)
        prev_idx = lambda i: (jnp.maximum(tiles // per * i - 1, 0), 0, 0, 0)
        next_idx = lambda i: (jnp.minimum(tiles // per * (i + 1), n_tiles // per - 1), 0, 0, 0)

    def specs(c):
        cur = pl.BlockSpec((None, None, tiles, dil, rows, gw), lambda b, i: (c, b, i, 0, 0, 0))
        prev = pl.BlockSpec(halo_block, lambda b, i: (c, b) + prev_idx(i))
        nxt = pl.BlockSpec(halo_block, lambda b, i: (c, b) + next_idx(i))
        return prev, cur, nxt

    (_, q_spec, _), k_specs, v_specs = specs(3 * group), specs(3 * group + 1), specs(3 * group + 2)
    if dil == 1:
        shapes = [((n_tiles, 1, TILE, gw), BF16), ((n_tiles, 1, TILE, LANES), F32)]
    else:
        shapes = [((n_tiles, gw // LANES, TILE, LANES), F32), ((n_tiles, 1, TILE, LANES), F32)]
    return pl.pallas_call(
        functools.partial(_band_attn_kernel, dil=dil),
        name=f"band_attn_d{dil}",
        grid=(bsz, n_steps),
        in_specs=[q_spec, *k_specs, *v_specs, _resident(bias.shape), _resident((2, 1, LANES))],
        out_specs=[pl.BlockSpec((None, tiles) + shape[1:], lambda b, i: (b, i, 0, 0, 0))
                   for shape, _ in shapes],
        out_shape=[jax.ShapeDtypeStruct((bsz,) + shape, dtype) for shape, dtype in shapes],
        compiler_params=_params(2),
    )(view, view, view, view, view, view, view, bias, headmask)


OUT_TOKENS = 2 * TILE


def _attn_out_kernel(h_ref, o0_ref, o1_ref, o2_ref, s0_ref, s1_ref, s2_ref, z_ref, expand_ref,
                     wo_ref, out_ref):
    gw, tm = GROUP_WIDTH, OUT_TOKENS
    n_heads = HEADS_PER_GROUP

    def rows_of(ref):
        tiles, slabs = ref.shape[:2]
        return jnp.concatenate(
            [jnp.concatenate([ref[t, c].astype(F32) for c in range(slabs)], axis=1)
             for t in range(tiles)], axis=0)

    ms = [rows_of(ref) for ref in (s0_ref, s1_ref, s2_ref)]
    ls = [pltpu.roll(m, LANES - n_heads, axis=1) for m in ms]
    top = jnp.maximum(jnp.maximum(ms[0], ms[1]), ms[2])
    ws = [jnp.exp2(m - top) for m in ms]
    denom = ls[0] * ws[0] + ls[1] * ws[1] + ls[2] * ws[2]
    lane = lax.broadcasted_iota(jnp.int32, (tm, LANES), 1)
    denom = jnp.where(lane < n_heads, denom, 1.0)
    acc = h_ref[0]
    for g, o_ref in enumerate((o0_ref, o1_ref, o2_ref)):
        scale = ws[g] / denom
        hi = scale.astype(BF16)
        lo = (scale - hi.astype(F32)).astype(BF16)
        spread = jnp.dot(jnp.concatenate([hi, lo], axis=1), expand_ref[...],
                         preferred_element_type=F32)
        z = z_ref[0, :, g * gw:(g + 1) * gw].astype(F32)
        y = rows_of(o_ref) * spread * _silu(z)
        acc = acc + jnp.dot(y.astype(BF16), wo_ref[g], preferred_element_type=F32)
    out_ref[0] = acc


def _attn_out(h, outs, stats, z, w_out):
    bsz, s, d = h.shape
    gw, tm = GROUP_WIDTH, OUT_TOKENS
    assert s % tm == 0 and tm % TILE == 0 and w_out.shape == (N_GROUPS * gw, d)
    expand = jnp.asarray(
        np.arange(2 * LANES)[:, None] % LANES == np.arange(gw)[None, :] // HEAD_DIM, BF16)
    tok = lambda width: pl.BlockSpec((1, tm, width), lambda b, i: (b, i, 0))
    tiled = lambda a: pl.BlockSpec((None, tm // TILE) + a.shape[2:], lambda b, i: (b, i, 0, 0, 0))
    return pl.pallas_call(
        _attn_out_kernel,
        name="attn_out",
        grid=(bsz, s // tm),
        in_specs=[tok(d)] + [tiled(a) for a in (*outs, *stats)] + [
            tok(N_GROUPS * gw), _resident((2 * LANES, gw)), _resident((N_GROUPS, gw, d))],
        out_specs=tok(d),
        out_shape=jax.ShapeDtypeStruct(h.shape, h.dtype),
        compiler_params=_params(2),
    )(h, *outs, *stats, z, expand, w_out.astype(BF16).reshape(N_GROUPS, gw, d))


def _attn_layer(h, g, w_in, q_gain, k_gain, rel_table, w_out):
    qkv, z = _attn_proj(h, g, w_in, q_gain, k_gain)
    outs, stats = [], []
    for grp, (window, dil) in enumerate(DIL_PAIRS):
        assert (window // 2) // dil == HALF_WINDOW
        o, st = _band_attention(qkv, grp, dil, _band_bias(rel_table, grp, dil))
        outs.append(o)
        stats.append(st)
    return _attn_out(h, outs, stats, z, w_out)


def kernel(x, norm_g, conv_w_in, conv_kernel, conv_bias, conv_w_out, attn_w_in, q_norm_g,
           k_norm_g, attn_w_out, rel_bias_table):
    h = x
    for layer in range(norm_g.shape[0]):
        j = layer // 2
        if layer % 2 == 0:
            h = _conv_layer(h, norm_g[layer], conv_w_in[j], conv_kernel[j], conv_bias[j],
                            conv_w_out[j])
        else:
            h = _attn_layer(h, norm_g[layer], attn_w_in[j], q_norm_g[j], k_norm_g[j],
                            rel_bias_table, attn_w_out[j])
    return h
```

```python
import functools

import jax
import jax.numpy as jnp
import numpy as np
from jax import lax
from jax.experimental import pallas as pl
from jax.experimental.pallas import tpu as pltpu

EPS = 1e-6
HEAD_DIM = 64
HEADS_PER_GROUP = 8
GROUP_WIDTH = HEAD_DIM * HEADS_PER_GROUP
DIL_PAIRS = ((128, 1), (512, 4), (2048, 16))
N_GROUPS = len(DIL_PAIRS)
HALF_WINDOW = 64
REL_BUCKETS = 32
REL_MAX_DIST = 1024
MASKED = -1e30
LOG2_E = 1.4426950408889634

LANES = 128
MXU_DIM = 256
Q_BLOCK = 128
K_WINDOW = Q_BLOCK + 2 * HALF_WINDOW
TILE = 512
ATTN_TILES = 4
UNSPLIT_ROW_STRIDE = 4
VMEM_LIMIT_BYTES = 56 * 1024 * 1024

BF16 = jnp.bfloat16
F32 = jnp.float32


def _silu(z):
    half = 0.5 * z
    return half + half * jnp.tanh(half)


def _rmsnorm(x, g):
    ms = jnp.mean(x * x, axis=-1, keepdims=True)
    return x * lax.rsqrt(ms + EPS) * g


def _resident(shape):
    zeros = (0,) * len(shape)
    return pl.BlockSpec(shape, lambda *_: zeros, pipeline_mode=pl.Buffered(1))


def _params(n_grid_axes):
    return pltpu.CompilerParams(dimension_semantics=("arbitrary",) * n_grid_axes,
                                vmem_limit_bytes=VMEM_LIMIT_BYTES)


CONV_TOKENS = 1024
CONV_HALO = 8
CONV_CHUNK = 512


def _conv_layer_kernel(xp_ref, x_ref, xn_ref, g_ref, w_ref, cw_ref, cb_ref, wo_ref, o_ref,
                       shift_ref):
    i = pl.program_id(1)
    last = pl.num_programs(1) - 1
    tm, halo, ce = CONV_TOKENS, CONV_HALO, CONV_CHUNK
    e = wo_ref.shape[0]
    g = g_ref[...]
    x = x_ref[0]
    xp = jnp.where(i > 0, xp_ref[0], 0.0)
    xn = jnp.where(i < last, xn_ref[0], 0.0)
    hn32 = _rmsnorm(x, g)
    hn = hn32.astype(BF16)
    hne = jnp.concatenate([_rmsnorm(xp, g), hn32, _rmsnorm(xn, g)], axis=0).astype(BF16)
    o_ref[0] = x

    for j in range(e // ce):
        ch = slice(j * ce, (j + 1) * ce)

        def proj(lhs, part):
            return jnp.dot(lhs, w_ref[:, part * e + j * ce:part * e + (j + 1) * ce],
                           preferred_element_type=F32)

        p = proj(hne, 1) * proj(hne, 2)
        slabs = shift_ref.at[j % 2]
        for c in range(ce // LANES):
            slabs[c] = p[:, c * LANES:(c + 1) * LANES]

        def shifted(by):
            return jnp.concatenate([slabs[c, halo + by:halo + by + tm, :]
                                    for c in range(ce // LANES)], axis=1)

        conv = (cw_ref[0:1, ch] * shifted(-1) + cw_ref[1:2, ch] * p[halo:halo + tm]
                + cw_ref[2:3, ch] * shifted(1) + cb_ref[:, ch])
        y = proj(hn, 0) * conv * _silu(proj(hn, 3))
        o_ref[0] += jnp.dot(y.astype(BF16), wo_ref[ch, :], preferred_element_type=F32)


def _conv_layer(x, g, w_in, conv_w, conv_b, w_out):
    bsz, s, d = x.shape
    e = w_out.shape[0]
    tm, halo, ce = CONV_TOKENS, CONV_HALO, CONV_CHUNK
    assert s % tm == 0 and tm % halo == 0 and e % ce == 0 and w_in.shape == (d, 4 * e)
    per_tile = tm // halo
    n_halo_blocks = s // halo
    return pl.pallas_call(
        _conv_layer_kernel,
        name="conv_layer",
        grid=(bsz, s // tm),
        in_specs=[
            pl.BlockSpec((1, halo, d), lambda b, i: (b, jnp.maximum(i * per_tile - 1, 0), 0)),
            pl.BlockSpec((1, tm, d), lambda b, i: (b, i, 0)),
            pl.BlockSpec((1, halo, d),
                         lambda b, i: (b, jnp.minimum((i + 1) * per_tile, n_halo_blocks - 1), 0)),
            _resident((1, d)),
            _resident((d, 4 * e)),
            _resident((3, e)),
            _resident((1, e)),
            _resident((e, d)),
        ],
        out_specs=pl.BlockSpec((1, tm, d), lambda b, i: (b, i, 0)),
        out_shape=jax.ShapeDtypeStruct(x.shape, x.dtype),
        scratch_shapes=[pltpu.VMEM((2, ce // LANES, tm + 2 * halo, LANES), F32)],
        compiler_params=_params(2),
    )(x, x, x, g.reshape(1, d), w_in.astype(BF16), conv_w, conv_b.reshape(1, e),
      w_out.astype(BF16))


def _attn_proj_kernel(h_ref, g_ref, w_ref, gain_ref, headmean_ref, qkv_ref, z_ref, slab_ref,
                      slab2_ref):
    gw, tm = GROUP_WIDTH, TILE
    n_slabs = slab_ref.shape[0]
    n_qkv = 3 * N_GROUPS
    hn32 = _rmsnorm(h_ref[0], g_ref[...])
    hn = hn32.astype(BF16)
    for c in range(n_slabs):
        slab_ref[c] = hn32[:, c * LANES:(c + 1) * LANES]
    step = UNSPLIT_ROW_STRIDE

    def gather_rows(ref, starts, rows, stride):
        return jnp.concatenate(
            [jnp.concatenate([ref[c, pl.ds(start, rows, stride=stride), :] for start in starts],
                             axis=0) for c in range(n_slabs)], axis=1)

    by_step = gather_rows(slab_ref, range(step), tm // step, step)
    for grp, (_, dil) in enumerate(DIL_PAIRS):
        if dil == 1:
            hg = hn
        elif dil == step:
            hg = by_step.astype(BF16)
        else:
            assert dil == step * step
            for c in range(n_slabs):
                slab2_ref[c] = by_step[:, c * LANES:(c + 1) * LANES]
            starts = [(r % step) * (tm // step) + r // step for r in range(dil)]
            hg = gather_rows(slab2_ref, starts, tm // dil, dil // step).astype(BF16)
        for t in range(3):
            c = 3 * grp + t
            y = jnp.dot(hg, w_ref[:, c * gw:(c + 1) * gw], preferred_element_type=F32)
            if t < 2:
                sq = (y * y).astype(BF16)
                ms = jnp.concatenate(
                    [jnp.dot(sq[:, k:k + MXU_DIM], headmean_ref[...], preferred_element_type=F32)
                     for k in range(0, gw, MXU_DIM)], axis=1)
                y = y * lax.rsqrt(ms + EPS) * gain_ref[c]
            qkv_ref[c, 0] = y.astype(BF16)
    for c in range(N_GROUPS):
        z = jnp.dot(hn, w_ref[:, (n_qkv + c) * gw:(n_qkv + c + 1) * gw],
                    preferred_element_type=F32)
        z_ref[0, :, c * gw:(c + 1) * gw] = z.astype(BF16)


def _attn_proj(h, g, w_in, q_gain, k_gain):
    bsz, s, d = h.shape
    gw, tm = GROUP_WIDTH, TILE
    n_qkv = 3 * N_GROUPS
    assert s % tm == 0 and d % LANES == 0 and w_in.shape == (d, (n_qkv + N_GROUPS) * gw)
    gains = jnp.stack([q_gain.reshape(N_GROUPS, gw) * (HEAD_DIM ** -0.5 * LOG2_E),
                       k_gain.reshape(N_GROUPS, gw),
                       jnp.ones((N_GROUPS, gw), F32)], axis=1).reshape(n_qkv, 1, gw)
    head = np.arange(MXU_DIM) // HEAD_DIM
    headmean = jnp.asarray((head[:, None] == head[None, :]) / HEAD_DIM, BF16)
    return pl.pallas_call(
        _attn_proj_kernel,
        name="attn_proj",
        grid=(bsz, s // tm),
        in_specs=[
            pl.BlockSpec((1, tm, d), lambda b, i: (b, i, 0)),
            _resident((1, d)),
            _resident(w_in.shape),
            _resident((n_qkv, 1, gw)),
            _resident((MXU_DIM, MXU_DIM)),
        ],
        out_specs=[
            pl.BlockSpec((n_qkv, 1, tm, gw), lambda b, i: (0, b, i, 0)),
            pl.BlockSpec((1, tm, N_GROUPS * gw), lambda b, i: (b, i, 0)),
        ],
        out_shape=[
            jax.ShapeDtypeStruct((n_qkv, bsz, s, gw), BF16),
            jax.ShapeDtypeStruct((bsz, s, N_GROUPS * gw), BF16),
        ],
        scratch_shapes=[pltpu.VMEM((d // LANES, tm, LANES), F32)] * 2,
        compiler_params=_params(2),
    )(h, g.reshape(1, d), w_in.astype(BF16), gains, headmean)


def _class_row_spans(prev_ref, cur_ref, next_ref, start, stop):
    rows = cur_ref.shape[2]
    seg = cur_ref.shape[0] * rows
    spans, pos = [], start
    while pos < stop:
        if pos < 0:
            ref, base, limit = prev_ref, pos + HALF_WINDOW, 0
        elif pos >= seg:
            ref, base, limit = next_ref, pos - seg, stop
        else:
            ref, base, limit = cur_ref, pos, seg
        tile, first = divmod(base, ref.shape[2])
        n = min(min(stop, limit) - pos, ref.shape[2] - first)
        spans.append((ref, tile, first, n))
        pos += n
    return spans


def _band_attn_kernel(q_ref, kp_ref, kc_ref, kn_ref, vp_ref, vc_ref, vn_ref, bias_ref,
                      headmask_ref, o_ref, stat_ref, *, dil):
    hw, qb = HALF_WINDOW, Q_BLOCK
    n_heads = HEADS_PER_GROUP
    blocks_per_class = q_ref.shape[0] * q_ref.shape[2] // qb
    step, last_step = pl.program_id(1), pl.num_programs(1) - 1
    lane = lax.broadcasted_iota(jnp.int32, (qb, LANES), 1)
    low_half = lane < HEAD_DIM

    def gather(spans, r, cols):
        parts = [ref[tile, r, first:first + n, cols] for ref, tile, first, n in spans]
        return parts[0] if len(parts) == 1 else jnp.concatenate(parts, axis=0)

    def scatter(ref, spans, r, cols, value):
        done = 0
        for _, tile, first, n in spans:
            ref[tile, r, first:first + n, cols] = value[done:done + n]
            done += n

    for r in range(dil):
        for u in range(blocks_per_class):
            q_spans = _class_row_spans(None, q_ref, None, u * qb, (u + 1) * qb)
            k_spans = _class_row_spans(kp_ref, kc_ref, kn_ref, u * qb - hw, (u + 1) * qb + hw)
            v_spans = _class_row_spans(vp_ref, vc_ref, vn_ref, u * qb - hw, (u + 1) * qb + hw)
            variant = 1
            if u == 0:
                variant = jnp.where(step == 0, 0, variant)
            if u == blocks_per_class - 1:
                variant = jnp.where(step == last_step, 2, variant)
            stats = jnp.zeros((qb, LANES), F32)
            for pair in range(n_heads // 2):
                cols = slice(pair * LANES, (pair + 1) * LANES)
                q2 = gather(q_spans, r, cols)
                qq = jnp.concatenate([q2 * headmask_ref[0], q2 * headmask_ref[1]], axis=0)
                kw = gather(k_spans, r, cols)
                vw = jnp.concatenate([gather(v_spans, r, cols),
                                      jnp.ones((K_WINDOW, LANES), BF16)], axis=1)
                s = lax.dot_general(qq, kw, (((1,), (1,)), ((), ())),
                                    preferred_element_type=F32)
                s = s + bias_ref[variant, pair]
                m = jnp.max(s, axis=-1, keepdims=True)
                e = jnp.exp2(s - m)
                o2 = jnp.dot(e.astype(BF16), vw, preferred_element_type=F32)
                l = o2[:, LANES:]
                o = jnp.where(low_half, o2[:qb, :LANES], o2[qb:, :LANES]).astype(BF16)
                scatter(o_ref, q_spans, r, cols, o)
                for k in range(2):
                    head = 2 * pair + k
                    stats = jnp.where(lane == head, m[k * qb:(k + 1) * qb], stats)
                    stats = jnp.where(lane == n_heads + head, l[k * qb:(k + 1) * qb], stats)
            scatter(stat_ref, q_spans, r, slice(None), stats)


def _t5_bucket(rel):
    nb = REL_BUCKETS // 2
    ret = (rel > 0).astype(np.int32) * nb
    n = np.abs(rel)
    max_exact = nb // 2
    large = max_exact + (np.log(np.maximum(n, 1) / max_exact)
                         / np.log(REL_MAX_DIST / max_exact) * (nb - max_exact)).astype(np.int32)
    large = np.minimum(large, nb - 1)
    return ret + np.where(n < max_exact, n, large).astype(np.int32)


def _band_bias(rel_table, group, dil):
    hw, qb, kw = HALF_WINDOW, Q_BLOCK, K_WINDOW
    n_heads = HEADS_PER_GROUP
    buckets = _t5_bucket(np.arange(-hw, hw + 1) * dil)
    heads = slice(group * n_heads, (group + 1) * n_heads)
    per_offset = rel_table[buckets][:, heads].astype(F32).T * LOG2_E
    width = qb + kw
    pad = jnp.full((n_heads, qb - 1), MASKED, F32)
    vec = jnp.concatenate([pad, per_offset, pad, jnp.full((n_heads, 2), MASKED, F32)], axis=1)
    assert vec.shape[1] == width + 1
    skew = jnp.tile(vec, (1, qb))[:, :qb * width].reshape(n_heads, qb, width)
    mid = skew[:, :, qb - 1:qb - 1 + kw]
    kj = np.arange(kw)[None, None, :]
    first = jnp.where(kj >= hw, mid, MASKED)
    final = jnp.where(kj < hw + qb, mid, MASKED)
    return jnp.stack([first, mid, final]).reshape(3, n_heads // 2, 2 * qb, kw)


def _band_attention(qkv, group, dil, bias):
    n_qkv, bsz, s, gw = qkv.shape
    hw, qb, tiles = HALF_WINDOW, Q_BLOCK, ATTN_TILES
    rows = TILE // dil
    n_tiles = s // TILE
    n_steps = n_tiles // tiles
    assert s % (TILE * tiles) == 0 and (tiles * rows) % qb == 0 and n_steps >= 2
    assert rows % hw == 0 or hw % rows == 0
    view = qkv.reshape(n_qkv, bsz, n_tiles, dil, rows, gw)
    headmask = jnp.asarray(
        (np.arange(LANES)[None, :] // HEAD_DIM == np.arange(2)[:, None])[:, None, :], BF16)
    if rows >= hw:
        halo_block = (None, None, 1, dil, hw, gw)
        sub = rows // hw
        prev_idx = lambda i: (jnp.maximum(tiles * i - 1, 0), 0, sub - 1, 0)
        next_idx = lambda i: (jnp.minimum(tiles * (i + 1), n_tiles - 1), 0, 0, 0)
    else:
        per = hw // rows
        halo_block = (None, None, per, dil, rows, gw)
        prev_idx = lambda i: (jnp.maximum(tiles // per * i - 1, 0), 0, 0, 0)
        next_idx = lambda i: (jnp.minimum(tiles // per * (i + 1), n_tiles // per - 1), 0, 0, 0)

    def specs(c):
        cur = pl.BlockSpec((None, None, tiles, dil, rows, gw), lambda b, i: (c, b, i, 0, 0, 0))
        prev = pl.BlockSpec(halo_block, lambda b, i: (c, b) + prev_idx(i))
        nxt = pl.BlockSpec(halo_block, lambda b, i: (c, b) + next_idx(i))
        return prev, cur, nxt

    (_, q_spec, _), k_specs, v_specs = specs(3 * group), specs(3 * group + 1), specs(3 * group + 2)
    out_block = lambda width: pl.BlockSpec((None, tiles, dil, rows, width),
                                           lambda b, i: (b, i, 0, 0, 0))
    o, stats = pl.pallas_call(
        functools.partial(_band_attn_kernel, dil=dil),
        name=f"band_attn_d{dil}",
        grid=(bsz, n_steps),
        in_specs=[q_spec, *k_specs, *v_specs, _resident(bias.shape), _resident((2, 1, LANES))],
        out_specs=[out_block(gw), out_block(LANES)],
        out_shape=[jax.ShapeDtypeStruct((bsz, n_tiles, dil, rows, gw), BF16),
                   jax.ShapeDtypeStruct((bsz, n_tiles, dil, rows, LANES), F32)],
        compiler_params=_params(2),
    )(view, view, view, view, view, view, view, bias, headmask)
    return o.reshape(bsz, s, gw), stats.reshape(bsz, s, LANES)


OUT_TOKENS = 2 * TILE


def _attn_out_kernel(h_ref, o0_ref, o1_ref, o2_ref, s0_ref, s1_ref, s2_ref, z_ref, expand_ref,
                     wo_ref, out_ref, slab_ref):
    gw, tm = GROUP_WIDTH, OUT_TOKENS
    n_heads = HEADS_PER_GROUP

    def restride(x, moves, stride):
        n = x.shape[1] // LANES
        for c in range(n):
            for base in range(0, tm, TILE):
                for src, dst, rows in moves:
                    slab_ref[c, pl.ds(base + dst, rows, stride=stride), :] = (
                        x[base + src:base + src + rows, c * LANES:(c + 1) * LANES])
        return jnp.concatenate([slab_ref[c] for c in range(n)], axis=1)

    def to_token_order(x, dil):
        step = UNSPLIT_ROW_STRIDE
        if dil > step:
            assert dil == step * step
            rows, block = TILE // dil, TILE // step
            x = restride(x, [((r1 + step * r2) * rows, r1 * block + r2, rows)
                             for r1 in range(step) for r2 in range(dil // step)], dil // step)
        if dil > 1:
            rows = TILE // step
            x = restride(x, [(r * rows, r, rows) for r in range(step)], step)
        return x

    dils = [dil for _, dil in DIL_PAIRS]
    ms = [to_token_order(ref[0], dil) for ref, dil in zip((s0_ref, s1_ref, s2_ref), dils)]
    ls = [pltpu.roll(m, LANES - n_heads, axis=1) for m in ms]
    top = jnp.maximum(jnp.maximum(ms[0], ms[1]), ms[2])
    ws = [jnp.exp2(m - top) for m in ms]
    denom = ls[0] * ws[0] + ls[1] * ws[1] + ls[2] * ws[2]
    lane = lax.broadcasted_iota(jnp.int32, (tm, LANES), 1)
    denom = jnp.where(lane < n_heads, denom, 1.0)
    acc = h_ref[0]
    for g, o_ref in enumerate((o0_ref, o1_ref, o2_ref)):
        scale = ws[g] / denom
        hi = scale.astype(BF16)
        lo = (scale - hi.astype(F32)).astype(BF16)
        spread = jnp.dot(jnp.concatenate([hi, lo], axis=1), expand_ref[...],
                         preferred_element_type=F32)
        z = z_ref[0, :, g * gw:(g + 1) * gw].astype(F32)
        y = to_token_order(o_ref[0].astype(F32), dils[g]) * spread * _silu(z)
        acc = acc + jnp.dot(y.astype(BF16), wo_ref[g], preferred_element_type=F32)
    out_ref[0] = acc


def _attn_out(h, outs, stats, z, w_out):
    bsz, s, d = h.shape
    gw, tm = GROUP_WIDTH, OUT_TOKENS
    assert s % tm == 0 and tm % TILE == 0 and w_out.shape == (N_GROUPS * gw, d)
    expand = jnp.asarray(
        np.arange(2 * LANES)[:, None] % LANES == np.arange(gw)[None, :] // HEAD_DIM, BF16)
    tok = lambda width: pl.BlockSpec((1, tm, width), lambda b, i: (b, i, 0))
    return pl.pallas_call(
        _attn_out_kernel,
        name="attn_out",
        grid=(bsz, s // tm),
        in_specs=[tok(d)] + [tok(gw)] * 3 + [tok(LANES)] * 3 + [
            tok(N_GROUPS * gw), _resident((2 * LANES, gw)), _resident((N_GROUPS, gw, d))],
        out_specs=tok(d),
        out_shape=jax.ShapeDtypeStruct(h.shape, h.dtype),
        scratch_shapes=[pltpu.VMEM((gw // LANES, tm, LANES), F32)],
        compiler_params=_params(2),
    )(h, *outs, *stats, z, expand, w_out.astype(BF16).reshape(N_GROUPS, gw, d))


def _attn_layer(h, g, w_in, q_gain, k_gain, rel_table, w_out):
    qkv, z = _attn_proj(h, g, w_in, q_gain, k_gain)
    outs, stats = [], []
    for grp, (window, dil) in enumerate(DIL_PAIRS):
        assert (window // 2) // dil == HALF_WINDOW
        o, st = _band_attention(qkv, grp, dil, _band_bias(rel_table, grp, dil))
        outs.append(o)
        stats.append(st)
    return _attn_out(h, outs, stats, z, w_out)


def kernel(x, norm_g, conv_w_in, conv_kernel, conv_bias, conv_w_out, attn_w_in, q_norm_g,
           k_norm_g, attn_w_out, rel_bias_table):
    h = x
    for layer in range(norm_g.shape[0]):
        j = layer // 2
        if layer % 2 == 0:
            h = _conv_layer(h, norm_g[layer], conv_w_in[j], conv_kernel[j], conv_bias[j],
                            conv_w_out[j])
        else:
            h = _attn_layer(h, norm_g[layer], attn_w_in[j], q_norm_g[j], k_norm_g[j],
                            rel_bias_table, attn_w_out[j])
    return h
```

```python
import functools

import jax
import jax.numpy as jnp
import numpy as np
from jax import lax
from jax.experimental import pallas as pl
from jax.experimental.pallas import tpu as pltpu

EPS = 1e-6
HEAD_DIM = 64
HEADS_PER_GROUP = 8
GROUP_WIDTH = HEAD_DIM * HEADS_PER_GROUP
DIL_PAIRS = ((128, 1), (512, 4), (2048, 16))
N_GROUPS = len(DIL_PAIRS)
HALF_WINDOW = 64
REL_BUCKETS = 32
REL_MAX_DIST = 1024
MASKED = -1e30
LOG2_E = 1.4426950408889634

LANES = 128
BF16_SUBLANES = 16
MXU_DIM = 256
Q_BLOCK = 128
K_WINDOW = Q_BLOCK + 2 * HALF_WINDOW
TILE = 512
ATTN_TILES = 4
UNSPLIT_ROW_STRIDE = 4
VMEM_LIMIT_BYTES = 56 * 1024 * 1024

BF16 = jnp.bfloat16
F32 = jnp.float32


def _silu(z):
    half = 0.5 * z
    return half + half * jnp.tanh(half)


def _rmsnorm(x, g):
    ms = jnp.mean(x * x, axis=-1, keepdims=True)
    return x * lax.rsqrt(ms + EPS) * g


def _resident(shape):
    zeros = (0,) * len(shape)
    return pl.BlockSpec(shape, lambda *_: zeros, pipeline_mode=pl.Buffered(1))


def _params(n_grid_axes):
    return pltpu.CompilerParams(dimension_semantics=("arbitrary",) * n_grid_axes,
                                vmem_limit_bytes=VMEM_LIMIT_BYTES)


CONV_TOKENS = 1024
CONV_HALO = 8
CONV_CHUNK = 512


def _conv_layer_kernel(xp_ref, x_ref, xn_ref, g_ref, w_ref, cw_ref, cb_ref, wo_ref, *rest):
    n_cast = (len(rest) - 2) // 2
    o_ref, shift_ref = rest[n_cast], rest[-1]
    for src_ref, dst_ref in zip(rest[:n_cast], rest[n_cast + 1:-1]):
        dst_ref[...] = src_ref[...].astype(BF16)
    i = pl.program_id(1)
    last = pl.num_programs(1) - 1
    tm, halo, ce = CONV_TOKENS, CONV_HALO, CONV_CHUNK
    e = wo_ref.shape[0]
    g = g_ref[...]
    x = x_ref[0]
    xp = jnp.where(i > 0, xp_ref[0], 0.0)
    xn = jnp.where(i < last, xn_ref[0], 0.0)
    hn32 = _rmsnorm(x, g)
    hn = hn32.astype(BF16)
    hne = jnp.concatenate([_rmsnorm(xp, g), hn32, _rmsnorm(xn, g)], axis=0).astype(BF16)
    o_ref[0] = x

    for j in range(e // ce):
        ch = slice(j * ce, (j + 1) * ce)

        def proj(lhs, part):
            return jnp.dot(lhs, w_ref[:, part * e + j * ce:part * e + (j + 1) * ce],
                           preferred_element_type=F32)

        p = proj(hne, 1) * proj(hne, 2)
        slabs = shift_ref.at[j % 2]
        for c in range(ce // LANES):
            slabs[c] = p[:, c * LANES:(c + 1) * LANES]

        def shifted(by):
            return jnp.concatenate([slabs[c, halo + by:halo + by + tm, :]
                                    for c in range(ce // LANES)], axis=1)

        conv = (cw_ref[0:1, ch] * shifted(-1) + cw_ref[1:2, ch] * p[halo:halo + tm]
                + cw_ref[2:3, ch] * shifted(1) + cb_ref[:, ch])
        y = proj(hn, 0) * conv * _silu(proj(hn, 3))
        o_ref[0] += jnp.dot(y.astype(BF16), wo_ref[ch, :], preferred_element_type=F32)


def _conv_layer(x, g, w_in, conv_w, conv_b, w_out, to_bf16=()):
    bsz, s, d = x.shape
    e = w_out.shape[0]
    tm, halo, ce = CONV_TOKENS, CONV_HALO, CONV_CHUNK
    assert s % tm == 0 and tm % halo == 0 and e % ce == 0 and w_in.shape == (d, 4 * e)
    per_tile = tm // halo
    n_halo_blocks = s // halo
    n_steps = bsz * (s // tm)
    assert all(a.shape[0] % (n_steps * BF16_SUBLANES) == 0 for a in to_bf16)
    cast_specs = [pl.BlockSpec((a.shape[0] // n_steps, a.shape[1]),
                               lambda b, i: (b * (s // tm) + i, 0)) for a in to_bf16]
    h, *copies = pl.pallas_call(
        _conv_layer_kernel,
        name="conv_layer",
        grid=(bsz, s // tm),
        in_specs=[
            pl.BlockSpec((1, halo, d), lambda b, i: (b, jnp.maximum(i * per_tile - 1, 0), 0)),
            pl.BlockSpec((1, tm, d), lambda b, i: (b, i, 0)),
            pl.BlockSpec((1, halo, d),
                         lambda b, i: (b, jnp.minimum((i + 1) * per_tile, n_halo_blocks - 1), 0)),
            _resident((1, d)),
            _resident((d, 4 * e)),
            _resident((3, e)),
            _resident((1, e)),
            _resident((e, d)),
            *cast_specs,
        ],
        out_specs=[pl.BlockSpec((1, tm, d), lambda b, i: (b, i, 0)), *cast_specs],
        out_shape=[jax.ShapeDtypeStruct(x.shape, x.dtype),
                   *[jax.ShapeDtypeStruct(a.shape, BF16) for a in to_bf16]],
        scratch_shapes=[pltpu.VMEM((2, ce // LANES, tm + 2 * halo, LANES), F32)],
        compiler_params=_params(2),
    )(x, x, x, g.reshape(1, d), w_in.astype(BF16), conv_w, conv_b.reshape(1, e),
      w_out.astype(BF16), *to_bf16)
    return h, copies


def _attn_proj_kernel(h_ref, g_ref, w_ref, gain_ref, headmean_ref, qkv_ref, z_ref, slab_ref,
                      slab2_ref):
    gw, tm = GROUP_WIDTH, TILE
    n_slabs = slab_ref.shape[0]
    n_qkv = 3 * N_GROUPS
    hn32 = _rmsnorm(h_ref[0], g_ref[...])
    hn = hn32.astype(BF16)
    for c in range(n_slabs):
        slab_ref[c] = hn32[:, c * LANES:(c + 1) * LANES]
    step = UNSPLIT_ROW_STRIDE

    def gather_rows(ref, starts, rows, stride):
        return jnp.concatenate(
            [jnp.concatenate([ref[c, pl.ds(start, rows, stride=stride), :] for start in starts],
                             axis=0) for c in range(n_slabs)], axis=1)

    by_step = gather_rows(slab_ref, range(step), tm // step, step)
    for grp, (_, dil) in enumerate(DIL_PAIRS):
        if dil == 1:
            hg = hn
        elif dil == step:
            hg = by_step.astype(BF16)
        else:
            assert dil == step * step
            for c in range(n_slabs):
                slab2_ref[c] = by_step[:, c * LANES:(c + 1) * LANES]
            starts = [(r % step) * (tm // step) + r // step for r in range(dil)]
            hg = gather_rows(slab2_ref, starts, tm // dil, dil // step).astype(BF16)
        for t in range(3):
            c = 3 * grp + t
            y = jnp.dot(hg, w_ref[:, c * gw:(c + 1) * gw], preferred_element_type=F32)
            if t < 2:
                sq = (y * y).astype(BF16)
                ms = jnp.concatenate(
                    [jnp.dot(sq[:, k:k + MXU_DIM], headmean_ref[...], preferred_element_type=F32)
                     for k in range(0, gw, MXU_DIM)], axis=1)
                y = y * lax.rsqrt(ms + EPS) * gain_ref[c]
            qkv_ref[c, 0] = y.astype(BF16)
    for c in range(N_GROUPS):
        z = jnp.dot(hn, w_ref[:, (n_qkv + c) * gw:(n_qkv + c + 1) * gw],
                    preferred_element_type=F32)
        z_ref[0, :, c * gw:(c + 1) * gw] = z.astype(BF16)


def _attn_proj(h, g, w_in, q_gain, k_gain):
    bsz, s, d = h.shape
    gw, tm = GROUP_WIDTH, TILE
    n_qkv = 3 * N_GROUPS
    assert s % tm == 0 and d % LANES == 0 and w_in.shape == (d, (n_qkv + N_GROUPS) * gw)
    gains = jnp.stack([q_gain.reshape(N_GROUPS, gw) * (HEAD_DIM ** -0.5 * LOG2_E),
                       k_gain.reshape(N_GROUPS, gw),
                       jnp.ones((N_GROUPS, gw), F32)], axis=1).reshape(n_qkv, 1, gw)
    head = np.arange(MXU_DIM) // HEAD_DIM
    headmean = jnp.asarray((head[:, None] == head[None, :]) / HEAD_DIM, BF16)
    return pl.pallas_call(
        _attn_proj_kernel,
        name="attn_proj",
        grid=(bsz, s // tm),
        in_specs=[
            pl.BlockSpec((1, tm, d), lambda b, i: (b, i, 0)),
            _resident((1, d)),
            _resident(w_in.shape),
            _resident((n_qkv, 1, gw)),
            _resident((MXU_DIM, MXU_DIM)),
        ],
        out_specs=[
            pl.BlockSpec((n_qkv, 1, tm, gw), lambda b, i: (0, b, i, 0)),
            pl.BlockSpec((1, tm, N_GROUPS * gw), lambda b, i: (b, i, 0)),
        ],
        out_shape=[
            jax.ShapeDtypeStruct((n_qkv, bsz, s, gw), BF16),
            jax.ShapeDtypeStruct((bsz, s, N_GROUPS * gw), BF16),
        ],
        scratch_shapes=[pltpu.VMEM((d // LANES, tm, LANES), F32)] * 2,
        compiler_params=_params(2),
    )(h, g.reshape(1, d), w_in.astype(BF16), gains, headmean)


def _class_row_spans(prev_ref, cur_ref, next_ref, start, stop):
    rows = cur_ref.shape[2]
    seg = cur_ref.shape[0] * rows
    spans, pos = [], start
    while pos < stop:
        if pos < 0:
            ref, base, limit = prev_ref, pos + HALF_WINDOW, 0
        elif pos >= seg:
            ref, base, limit = next_ref, pos - seg, stop
        else:
            ref, base, limit = cur_ref, pos, seg
        tile, first = divmod(base, ref.shape[2])
        n = min(min(stop, limit) - pos, ref.shape[2] - first)
        spans.append((ref, tile, first, n))
        pos += n
    return spans


def _band_attn_kernel(q_ref, kp_ref, kc_ref, kn_ref, vp_ref, vc_ref, vn_ref, bias_ref,
                      headmask_ref, o_ref, stat_ref, *, dil):
    hw, qb = HALF_WINDOW, Q_BLOCK
    n_heads = HEADS_PER_GROUP
    blocks_per_class = q_ref.shape[0] * q_ref.shape[2] // qb
    step, last_step = pl.program_id(1), pl.num_programs(1) - 1
    lane = lax.broadcasted_iota(jnp.int32, (qb, LANES), 1)
    low_half = lane < HEAD_DIM

    def gather(spans, r, cols):
        parts = [ref[tile, r, first:first + n, cols] for ref, tile, first, n in spans]
        return parts[0] if len(parts) == 1 else jnp.concatenate(parts, axis=0)

    def scatter(ref, spans, r, cols, value):
        done = 0
        for _, tile, first, n in spans:
            ref[tile, r, first:first + n, cols] = value[done:done + n]
            done += n

    for r in range(dil):
        for u in range(blocks_per_class):
            q_spans = _class_row_spans(None, q_ref, None, u * qb, (u + 1) * qb)
            k_spans = _class_row_spans(kp_ref, kc_ref, kn_ref, u * qb - hw, (u + 1) * qb + hw)
            v_spans = _class_row_spans(vp_ref, vc_ref, vn_ref, u * qb - hw, (u + 1) * qb + hw)
            variant = 1
            if u == 0:
                variant = jnp.where(step == 0, 0, variant)
            if u == blocks_per_class - 1:
                variant = jnp.where(step == last_step, 2, variant)
            stats = jnp.zeros((qb, LANES), F32)
            for pair in range(n_heads // 2):
                cols = slice(pair * LANES, (pair + 1) * LANES)
                q2 = gather(q_spans, r, cols)
                qq = jnp.concatenate([q2 * headmask_ref[0], q2 * headmask_ref[1]], axis=0)
                kw = gather(k_spans, r, cols)
                vw = jnp.concatenate([gather(v_spans, r, cols),
                                      jnp.ones((K_WINDOW, LANES), BF16)], axis=1)
                s = lax.dot_general(qq, kw, (((1,), (1,)), ((), ())),
                                    preferred_element_type=F32)
                s = s + bias_ref[variant, pair]
                m = jnp.max(s, axis=-1, keepdims=True)
                e = jnp.exp2(s - m)
                o2 = jnp.dot(e.astype(BF16), vw, preferred_element_type=F32)
                l = o2[:, LANES:]
                o = jnp.where(low_half, o2[:qb, :LANES], o2[qb:, :LANES]).astype(BF16)
                scatter(o_ref, q_spans, r, cols, o)
                for k in range(2):
                    head = 2 * pair + k
                    stats = jnp.where(lane == head, m[k * qb:(k + 1) * qb], stats)
                    stats = jnp.where(lane == n_heads + head, l[k * qb:(k + 1) * qb], stats)
            scatter(stat_ref, q_spans, r, slice(None), stats)


def _t5_bucket(rel):
    nb = REL_BUCKETS // 2
    ret = (rel > 0).astype(np.int32) * nb
    n = np.abs(rel)
    max_exact = nb // 2
    large = max_exact + (np.log(np.maximum(n, 1) / max_exact)
                         / np.log(REL_MAX_DIST / max_exact) * (nb - max_exact)).astype(np.int32)
    large = np.minimum(large, nb - 1)
    return ret + np.where(n < max_exact, n, large).astype(np.int32)


def _band_bias(rel_table, group, dil):
    hw, qb, kw = HALF_WINDOW, Q_BLOCK, K_WINDOW
    n_heads = HEADS_PER_GROUP
    buckets = _t5_bucket(np.arange(-hw, hw + 1) * dil)
    heads = slice(group * n_heads, (group + 1) * n_heads)
    per_offset = rel_table[buckets][:, heads].astype(F32).T * LOG2_E
    width = qb + kw
    pad = jnp.full((n_heads, qb - 1), MASKED, F32)
    vec = jnp.concatenate([pad, per_offset, pad, jnp.full((n_heads, 2), MASKED, F32)], axis=1)
    assert vec.shape[1] == width + 1
    skew = jnp.tile(vec, (1, qb))[:, :qb * width].reshape(n_heads, qb, width)
    mid = skew[:, :, qb - 1:qb - 1 + kw]
    kj = np.arange(kw)[None, None, :]
    first = jnp.where(kj >= hw, mid, MASKED)
    final = jnp.where(kj < hw + qb, mid, MASKED)
    return jnp.stack([first, mid, final]).reshape(3, n_heads // 2, 2 * qb, kw)


def _band_attention(qkv, group, dil, bias):
    n_qkv, bsz, s, gw = qkv.shape
    hw, qb, tiles = HALF_WINDOW, Q_BLOCK, ATTN_TILES
    rows = TILE // dil
    n_tiles = s // TILE
    n_steps = n_tiles // tiles
    assert s % (TILE * tiles) == 0 and (tiles * rows) % qb == 0 and n_steps >= 2
    assert rows % hw == 0 or hw % rows == 0
    view = qkv.reshape(n_qkv, bsz, n_tiles, dil, rows, gw)
    headmask = jnp.asarray(
        (np.arange(LANES)[None, :] // HEAD_DIM == np.arange(2)[:, None])[:, None, :], BF16)
    if rows >= hw:
        halo_block = (None, None, 1, dil, hw, gw)
        sub = rows // hw
        prev_idx = lambda i: (jnp.maximum(tiles * i - 1, 0), 0, sub - 1, 0)
        next_idx = lambda i: (jnp.minimum(tiles * (i + 1), n_tiles - 1), 0, 0, 0)
    else:
        per = hw // rows
        halo_block = (None, None, per, dil, rows, gw)
        prev_idx = lambda i: (jnp.maximum(tiles // per * i - 1, 0), 0, 0, 0)
        next_idx = lambda i: (jnp.minimum(tiles // per * (i + 1), n_tiles // per - 1), 0, 0, 0)

    def specs(c):
        cur = pl.BlockSpec((None, None, tiles, dil, rows, gw), lambda b, i: (c, b, i, 0, 0, 0))
        prev = pl.BlockSpec(halo_block, lambda b, i: (c, b) + prev_idx(i))
        nxt = pl.BlockSpec(halo_block, lambda b, i: (c, b) + next_idx(i))
        return prev, cur, nxt

    (_, q_spec, _), k_specs, v_specs = specs(3 * group), specs(3 * group + 1), specs(3 * group + 2)
    out_block = lambda width: pl.BlockSpec((None, tiles, dil, rows, width),
                                           lambda b, i: (b, i, 0, 0, 0))
    o, stats = pl.pallas_call(
        functools.partial(_band_attn_kernel, dil=dil),
        name=f"band_attn_d{dil}",
        grid=(bsz, n_steps),
        in_specs=[q_spec, *k_specs, *v_specs, _resident(bias.shape), _resident((2, 1, LANES))],
        out_specs=[out_block(gw), out_block(LANES)],
        out_shape=[jax.ShapeDtypeStruct((bsz, n_tiles, dil, rows, gw), BF16),
                   jax.ShapeDtypeStruct((bsz, n_tiles, dil, rows, LANES), F32)],
        compiler_params=_params(2),
    )(view, view, view, view, view, view, view, bias, headmask)
    return o.reshape(bsz, s, gw), stats.reshape(bsz, s, LANES)


OUT_TOKENS = 2 * TILE


def _attn_out_kernel(h_ref, o0_ref, o1_ref, o2_ref, s0_ref, s1_ref, s2_ref, z_ref, expand_ref,
                     wo_ref, out_ref, slab_ref):
    gw, tm = GROUP_WIDTH, OUT_TOKENS
    n_heads = HEADS_PER_GROUP

    def restride(x, moves, stride):
        n = x.shape[1] // LANES
        for c in range(n):
            for base in range(0, tm, TILE):
                for src, dst, rows in moves:
                    slab_ref[c, pl.ds(base + dst, rows, stride=stride), :] = (
                        x[base + src:base + src + rows, c * LANES:(c + 1) * LANES])
        return jnp.concatenate([slab_ref[c] for c in range(n)], axis=1)

    def to_token_order(x, dil):
        step = UNSPLIT_ROW_STRIDE
        if dil > step:
            assert dil == step * step
            rows, block = TILE // dil, TILE // step
            x = restride(x, [((r1 + step * r2) * rows, r1 * block + r2, rows)
                             for r1 in range(step) for r2 in range(dil // step)], dil // step)
        if dil > 1:
            rows = TILE // step
            x = restride(x, [(r * rows, r, rows) for r in range(step)], step)
        return x

    dils = [dil for _, dil in DIL_PAIRS]
    ms = [to_token_order(ref[0], dil) for ref, dil in zip((s0_ref, s1_ref, s2_ref), dils)]
    ls = [pltpu.roll(m, LANES - n_heads, axis=1) for m in ms]
    top = jnp.maximum(jnp.maximum(ms[0], ms[1]), ms[2])
    ws = [jnp.exp2(m - top) for m in ms]
    denom = ls[0] * ws[0] + ls[1] * ws[1] + ls[2] * ws[2]
    lane = lax.broadcasted_iota(jnp.int32, (tm, LANES), 1)
    denom = jnp.where(lane < n_heads, denom, 1.0)
    acc = h_ref[0]
    for g, o_ref in enumerate((o0_ref, o1_ref, o2_ref)):
        scale = ws[g] / denom
        hi = scale.astype(BF16)
        lo = (scale - hi.astype(F32)).astype(BF16)
        spread = jnp.dot(jnp.concatenate([hi, lo], axis=1), expand_ref[...],
                         preferred_element_type=F32)
        z = z_ref[0, :, g * gw:(g + 1) * gw].astype(F32)
        y = to_token_order(o_ref[0].astype(F32), dils[g]) * spread * _silu(z)
        acc = acc + jnp.dot(y.astype(BF16), wo_ref[g], preferred_element_type=F32)
    out_ref[0] = acc


def _attn_out(h, outs, stats, z, w_out):
    bsz, s, d = h.shape
    gw, tm = GROUP_WIDTH, OUT_TOKENS
    assert s % tm == 0 and tm % TILE == 0 and w_out.shape == (N_GROUPS * gw, d)
    expand = jnp.asarray(
        np.arange(2 * LANES)[:, None] % LANES == np.arange(gw)[None, :] // HEAD_DIM, BF16)
    tok = lambda width: pl.BlockSpec((1, tm, width), lambda b, i: (b, i, 0))
    return pl.pallas_call(
        _attn_out_kernel,
        name="attn_out",
        grid=(bsz, s // tm),
        in_specs=[tok(d)] + [tok(gw)] * 3 + [tok(LANES)] * 3 + [
            tok(N_GROUPS * gw), _resident((2 * LANES, gw)), _resident((N_GROUPS, gw, d))],
        out_specs=tok(d),
        out_shape=jax.ShapeDtypeStruct(h.shape, h.dtype),
        scratch_shapes=[pltpu.VMEM((gw // LANES, tm, LANES), F32)],
        compiler_params=_params(2),
    )(h, *outs, *stats, z, expand, w_out.astype(BF16).reshape(N_GROUPS, gw, d))


def _attn_layer(h, g, w_in, q_gain, k_gain, rel_table, w_out):
    qkv, z = _attn_proj(h, g, w_in, q_gain, k_gain)
    outs, stats = [], []
    for grp, (window, dil) in enumerate(DIL_PAIRS):
        assert (window // 2) // dil == HALF_WINDOW
        o, st = _band_attention(qkv, grp, dil, _band_bias(rel_table, grp, dil))
        outs.append(o)
        stats.append(st)
    return _attn_out(h, outs, stats, z, w_out)


def kernel(x, norm_g, conv_w_in, conv_kernel, conv_bias, conv_w_out, attn_w_in, q_norm_g,
           k_norm_g, attn_w_out, rel_bias_table):
    h = x
    depth = norm_g.shape[0]
    for layer in range(depth):
        j = layer // 2
        if layer % 2 == 0:
            nxt = (attn_w_in[j], attn_w_out[j]) if layer + 1 < depth else ()
            h, attn_weights = _conv_layer(h, norm_g[layer], conv_w_in[j], conv_kernel[j],
                                          conv_bias[j], conv_w_out[j], nxt)
        else:
            w_in, w_out = attn_weights
            h = _attn_layer(h, norm_g[layer], w_in, q_norm_g[j], k_norm_g[j], rel_bias_table,
                            w_out)
    return h
```

```python
import functools

import jax
import jax.numpy as jnp
import numpy as np
from jax import lax
from jax.experimental import pallas as pl
from jax.experimental.pallas import tpu as pltpu

EPS = 1e-6
HEAD_DIM = 64
HEADS_PER_GROUP = 8
GROUP_WIDTH = HEAD_DIM * HEADS_PER_GROUP
DIL_PAIRS = ((128, 1), (512, 4), (2048, 16))
N_GROUPS = len(DIL_PAIRS)
HALF_WINDOW = 64
REL_BUCKETS = 32
REL_MAX_DIST = 1024
MASKED = -1e30
LOG2_E = 1.4426950408889634

LANES = 128
BF16_SUBLANES = 16
MXU_DIM = 256
Q_BLOCK = 128
K_WINDOW = Q_BLOCK + 2 * HALF_WINDOW
TILE = 512
ATTN_TILES = 8
UNSPLIT_ROW_STRIDE = 4
VMEM_LIMIT_BYTES = 56 * 1024 * 1024

BF16 = jnp.bfloat16
F32 = jnp.float32


def _silu(z):
    half = 0.5 * z
    return half + half * jnp.tanh(half)


def _rmsnorm(x, g):
    ms = jnp.mean(x * x, axis=-1, keepdims=True)
    return x * lax.rsqrt(ms + EPS) * g


def _resident(shape):
    zeros = (0,) * len(shape)
    return pl.BlockSpec(shape, lambda *_: zeros, pipeline_mode=pl.Buffered(1))


def _params(n_grid_axes):
    return pltpu.CompilerParams(dimension_semantics=("arbitrary",) * n_grid_axes,
                                vmem_limit_bytes=VMEM_LIMIT_BYTES)


CONV_TOKENS = 1024
CONV_HALO = 8
CONV_CHUNK = 512


def _conv_layer_kernel(xp_ref, x_ref, xn_ref, g_ref, w_ref, cw_ref, cb_ref, wo_ref, *rest):
    n_cast = (len(rest) - 2) // 2
    o_ref, shift_ref = rest[n_cast], rest[-1]
    for src_ref, dst_ref in zip(rest[:n_cast], rest[n_cast + 1:-1]):
        dst_ref[...] = src_ref[...].astype(BF16)
    i = pl.program_id(1)
    last = pl.num_programs(1) - 1
    tm, halo, ce = CONV_TOKENS, CONV_HALO, CONV_CHUNK
    e = wo_ref.shape[0]
    g = g_ref[...]
    x = x_ref[0]
    xp = jnp.where(i > 0, xp_ref[0], 0.0)
    xn = jnp.where(i < last, xn_ref[0], 0.0)
    hn32 = _rmsnorm(x, g)
    hn = hn32.astype(BF16)
    hne = jnp.concatenate([_rmsnorm(xp, g), hn32, _rmsnorm(xn, g)], axis=0).astype(BF16)
    o_ref[0] = x

    for j in range(e // ce):
        ch = slice(j * ce, (j + 1) * ce)

        def proj(lhs, part):
            return jnp.dot(lhs, w_ref[:, part * e + j * ce:part * e + (j + 1) * ce],
                           preferred_element_type=F32)

        p = proj(hne, 1) * proj(hne, 2)
        slabs = shift_ref.at[j % 2]
        for c in range(ce // LANES):
            slabs[c] = p[:, c * LANES:(c + 1) * LANES]

        def shifted(by):
            return jnp.concatenate([slabs[c, halo + by:halo + by + tm, :]
                                    for c in range(ce // LANES)], axis=1)

        conv = (cw_ref[0:1, ch] * shifted(-1) + cw_ref[1:2, ch] * p[halo:halo + tm]
                + cw_ref[2:3, ch] * shifted(1) + cb_ref[:, ch])
        y = proj(hn, 0) * conv * _silu(proj(hn, 3))
        o_ref[0] += jnp.dot(y.astype(BF16), wo_ref[ch, :], preferred_element_type=F32)


def _conv_layer(x, g, w_in, conv_w, conv_b, w_out, to_bf16=()):
    bsz, s, d = x.shape
    e = w_out.shape[0]
    tm, halo, ce = CONV_TOKENS, CONV_HALO, CONV_CHUNK
    assert s % tm == 0 and tm % halo == 0 and e % ce == 0 and w_in.shape == (d, 4 * e)
    per_tile = tm // halo
    n_halo_blocks = s // halo
    n_steps = bsz * (s // tm)
    assert all(a.shape[0] % (n_steps * BF16_SUBLANES) == 0 for a in to_bf16)
    cast_specs = [pl.BlockSpec((a.shape[0] // n_steps, a.shape[1]),
                               lambda b, i: (b * (s // tm) + i, 0)) for a in to_bf16]
    h, *copies = pl.pallas_call(
        _conv_layer_kernel,
        name="conv_layer",
        grid=(bsz, s // tm),
        in_specs=[
            pl.BlockSpec((1, halo, d), lambda b, i: (b, jnp.maximum(i * per_tile - 1, 0), 0)),
            pl.BlockSpec((1, tm, d), lambda b, i: (b, i, 0)),
            pl.BlockSpec((1, halo, d),
                         lambda b, i: (b, jnp.minimum((i + 1) * per_tile, n_halo_blocks - 1), 0)),
            _resident((1, d)),
            _resident((d, 4 * e)),
            _resident((3, e)),
            _resident((1, e)),
            _resident((e, d)),
            *cast_specs,
        ],
        out_specs=[pl.BlockSpec((1, tm, d), lambda b, i: (b, i, 0)), *cast_specs],
        out_shape=[jax.ShapeDtypeStruct(x.shape, x.dtype),
                   *[jax.ShapeDtypeStruct(a.shape, BF16) for a in to_bf16]],
        scratch_shapes=[pltpu.VMEM((2, ce // LANES, tm + 2 * halo, LANES), F32)],
        compiler_params=_params(2),
    )(x, x, x, g.reshape(1, d), w_in.astype(BF16), conv_w, conv_b.reshape(1, e),
      w_out.astype(BF16), *to_bf16)
    return h, copies


def _attn_proj_kernel(h_ref, g_ref, w_ref, gain_ref, headmean_ref, qkv_ref, z_ref, slab_ref,
                      slab2_ref):
    gw, tm = GROUP_WIDTH, TILE
    n_slabs = slab_ref.shape[0]
    n_qkv = 3 * N_GROUPS
    hn32 = _rmsnorm(h_ref[0], g_ref[...])
    hn = hn32.astype(BF16)
    for c in range(n_slabs):
        slab_ref[c] = hn32[:, c * LANES:(c + 1) * LANES]
    step = UNSPLIT_ROW_STRIDE

    def gather_rows(ref, starts, rows, stride):
        return jnp.concatenate(
            [jnp.concatenate([ref[c, pl.ds(start, rows, stride=stride), :] for start in starts],
                             axis=0) for c in range(n_slabs)], axis=1)

    by_step = gather_rows(slab_ref, range(step), tm // step, step)
    for grp, (_, dil) in enumerate(DIL_PAIRS):
        if dil == 1:
            hg = hn
        elif dil == step:
            hg = by_step.astype(BF16)
        else:
            assert dil == step * step
            for c in range(n_slabs):
                slab2_ref[c] = by_step[:, c * LANES:(c + 1) * LANES]
            starts = [(r % step) * (tm // step) + r // step for r in range(dil)]
            hg = gather_rows(slab2_ref, starts, tm // dil, dil // step).astype(BF16)
        for t in range(3):
            c = 3 * grp + t
            y = jnp.dot(hg, w_ref[:, c * gw:(c + 1) * gw], preferred_element_type=F32)
            if t < 2:
                sq = (y * y).astype(BF16)
                ms = jnp.concatenate(
                    [jnp.dot(sq[:, k:k + MXU_DIM], headmean_ref[...], preferred_element_type=F32)
                     for k in range(0, gw, MXU_DIM)], axis=1)
                y = y * lax.rsqrt(ms + EPS) * gain_ref[c]
            qkv_ref[c, 0] = y.astype(BF16)
    for c in range(N_GROUPS):
        z = jnp.dot(hn, w_ref[:, (n_qkv + c) * gw:(n_qkv + c + 1) * gw],
                    preferred_element_type=F32)
        z_ref[0, :, c * gw:(c + 1) * gw] = z.astype(BF16)


def _attn_proj(h, g, w_in, q_gain, k_gain):
    bsz, s, d = h.shape
    gw, tm = GROUP_WIDTH, TILE
    n_qkv = 3 * N_GROUPS
    assert s % tm == 0 and d % LANES == 0 and w_in.shape == (d, (n_qkv + N_GROUPS) * gw)
    gains = jnp.stack([q_gain.reshape(N_GROUPS, gw) * (HEAD_DIM ** -0.5 * LOG2_E),
                       k_gain.reshape(N_GROUPS, gw),
                       jnp.ones((N_GROUPS, gw), F32)], axis=1).reshape(n_qkv, 1, gw)
    head = np.arange(MXU_DIM) // HEAD_DIM
    headmean = jnp.asarray((head[:, None] == head[None, :]) / HEAD_DIM, BF16)
    return pl.pallas_call(
        _attn_proj_kernel,
        name="attn_proj",
        grid=(bsz, s // tm),
        in_specs=[
            pl.BlockSpec((1, tm, d), lambda b, i: (b, i, 0)),
            _resident((1, d)),
            _resident(w_in.shape),
            _resident((n_qkv, 1, gw)),
            _resident((MXU_DIM, MXU_DIM)),
        ],
        out_specs=[
            pl.BlockSpec((n_qkv, 1, tm, gw), lambda b, i: (0, b, i, 0)),
            pl.BlockSpec((1, tm, N_GROUPS * gw), lambda b, i: (b, i, 0)),
        ],
        out_shape=[
            jax.ShapeDtypeStruct((n_qkv, bsz, s, gw), BF16),
            jax.ShapeDtypeStruct((bsz, s, N_GROUPS * gw), BF16),
        ],
        scratch_shapes=[pltpu.VMEM((d // LANES, tm, LANES), F32)] * 2,
        compiler_params=_params(2),
    )(h, g.reshape(1, d), w_in.astype(BF16), gains, headmean)


def _class_row_spans(prev_ref, cur_ref, next_ref, start, stop):
    rows = cur_ref.shape[2]
    seg = cur_ref.shape[0] * rows
    spans, pos = [], start
    while pos < stop:
        if pos < 0:
            ref, base, limit = prev_ref, pos + HALF_WINDOW, 0
        elif pos >= seg:
            ref, base, limit = next_ref, pos - seg, stop
        else:
            ref, base, limit = cur_ref, pos, seg
        tile, first = divmod(base, ref.shape[2])
        n = min(min(stop, limit) - pos, ref.shape[2] - first)
        spans.append((ref, tile, first, n))
        pos += n
    return spans


def _band_attn_kernel(q_ref, kp_ref, kc_ref, kn_ref, vp_ref, vc_ref, vn_ref, bias_ref,
                      headmask_ref, o_ref, stat_ref, *, dil):
    hw, qb = HALF_WINDOW, Q_BLOCK
    n_heads = HEADS_PER_GROUP
    blocks_per_class = q_ref.shape[0] * q_ref.shape[2] // qb
    step, last_step = pl.program_id(1), pl.num_programs(1) - 1
    lane = lax.broadcasted_iota(jnp.int32, (qb, LANES), 1)
    low_half = lane < HEAD_DIM

    def gather(spans, r, cols):
        parts = [ref[tile, r, first:first + n, cols] for ref, tile, first, n in spans]
        return parts[0] if len(parts) == 1 else jnp.concatenate(parts, axis=0)

    def scatter(ref, spans, r, cols, value):
        done = 0
        for _, tile, first, n in spans:
            ref[tile, r, first:first + n, cols] = value[done:done + n]
            done += n

    for r in range(dil):
        for u in range(blocks_per_class):
            q_spans = _class_row_spans(None, q_ref, None, u * qb, (u + 1) * qb)
            k_spans = _class_row_spans(kp_ref, kc_ref, kn_ref, u * qb - hw, (u + 1) * qb + hw)
            v_spans = _class_row_spans(vp_ref, vc_ref, vn_ref, u * qb - hw, (u + 1) * qb + hw)
            variant = 1
            if u == 0:
                variant = jnp.where(step == 0, 0, variant)
            if u == blocks_per_class - 1:
                variant = jnp.where(step == last_step, 2, variant)
            stats = jnp.zeros((qb, LANES), F32)
            for pair in range(n_heads // 2):
                cols = slice(pair * LANES, (pair + 1) * LANES)
                q2 = gather(q_spans, r, cols)
                qq = jnp.concatenate([q2 * headmask_ref[0], q2 * headmask_ref[1]], axis=0)
                kw = gather(k_spans, r, cols)
                vw = jnp.concatenate([gather(v_spans, r, cols),
                                      jnp.ones((K_WINDOW, LANES), BF16)], axis=1)
                s = lax.dot_general(qq, kw, (((1,), (1,)), ((), ())),
                                    preferred_element_type=F32)
                s = s + bias_ref[variant, pair]
                m = jnp.max(s, axis=-1, keepdims=True)
                e = jnp.exp2(s - m)
                o2 = jnp.dot(e.astype(BF16), vw, preferred_element_type=F32)
                l = o2[:, LANES:]
                o = jnp.where(low_half, o2[:qb, :LANES], o2[qb:, :LANES]).astype(BF16)
                scatter(o_ref, q_spans, r, cols, o)
                for k in range(2):
                    head = 2 * pair + k
                    stats = jnp.where(lane == head, m[k * qb:(k + 1) * qb], stats)
                    stats = jnp.where(lane == n_heads + head, l[k * qb:(k + 1) * qb], stats)
            scatter(stat_ref, q_spans, r, slice(None), stats)


def _t5_bucket(rel):
    nb = REL_BUCKETS // 2
    ret = (rel > 0).astype(np.int32) * nb
    n = np.abs(rel)
    max_exact = nb // 2
    large = max_exact + (np.log(np.maximum(n, 1) / max_exact)
                         / np.log(REL_MAX_DIST / max_exact) * (nb - max_exact)).astype(np.int32)
    large = np.minimum(large, nb - 1)
    return ret + np.where(n < max_exact, n, large).astype(np.int32)


def _band_bias(rel_table, group, dil):
    hw, qb, kw = HALF_WINDOW, Q_BLOCK, K_WINDOW
    n_heads = HEADS_PER_GROUP
    buckets = _t5_bucket(np.arange(-hw, hw + 1) * dil)
    heads = slice(group * n_heads, (group + 1) * n_heads)
    per_offset = rel_table[buckets][:, heads].astype(F32).T * LOG2_E
    width = qb + kw
    pad = jnp.full((n_heads, qb - 1), MASKED, F32)
    vec = jnp.concatenate([pad, per_offset, pad, jnp.full((n_heads, 2), MASKED, F32)], axis=1)
    assert vec.shape[1] == width + 1
    skew = jnp.tile(vec, (1, qb))[:, :qb * width].reshape(n_heads, qb, width)
    mid = skew[:, :, qb - 1:qb - 1 + kw]
    kj = np.arange(kw)[None, None, :]
    first = jnp.where(kj >= hw, mid, MASKED)
    final = jnp.where(kj < hw + qb, mid, MASKED)
    return jnp.stack([first, mid, final]).reshape(3, n_heads // 2, 2 * qb, kw)


def _band_attention(qkv, group, dil, bias):
    n_qkv, bsz, s, gw = qkv.shape
    hw, qb, tiles = HALF_WINDOW, Q_BLOCK, ATTN_TILES
    rows = TILE // dil
    n_tiles = s // TILE
    n_steps = n_tiles // tiles
    assert s % (TILE * tiles) == 0 and (tiles * rows) % qb == 0 and n_steps >= 2
    assert rows % hw == 0 or hw % rows == 0
    view = qkv.reshape(n_qkv, bsz, n_tiles, dil, rows, gw)
    headmask = jnp.asarray(
        (np.arange(LANES)[None, :] // HEAD_DIM == np.arange(2)[:, None])[:, None, :], BF16)
    if rows >= hw:
        halo_block = (None, None, 1, dil, hw, gw)
        sub = rows // hw
        prev_idx = lambda i: (jnp.maximum(tiles * i - 1, 0), 0, sub - 1, 0)
        next_idx = lambda i: (jnp.minimum(tiles * (i + 1), n_tiles - 1), 0, 0, 0)
    else:
        per = hw // rows
        halo_block = (None, None, per, dil, rows, gw)
        prev_idx = lambda i: (jnp.maximum(tiles // per * i - 1, 0), 0, 0, 0)
        next_idx = lambda i: (jnp.minimum(tiles // per * (i + 1), n_tiles // per - 1), 0, 0, 0)

    def specs(c):
        cur = pl.BlockSpec((None, None, tiles, dil, rows, gw), lambda b, i: (c, b, i, 0, 0, 0))
        prev = pl.BlockSpec(halo_block, lambda b, i: (c, b) + prev_idx(i))
        nxt = pl.BlockSpec(halo_block, lambda b, i: (c, b) + next_idx(i))
        return prev, cur, nxt

    (_, q_spec, _), k_specs, v_specs = specs(3 * group), specs(3 * group + 1), specs(3 * group + 2)
    out_block = lambda width: pl.BlockSpec((None, tiles, dil, rows, width),
                                           lambda b, i: (b, i, 0, 0, 0))
    o, stats = pl.pallas_call(
        functools.partial(_band_attn_kernel, dil=dil),
        name=f"band_attn_d{dil}",
        grid=(bsz, n_steps),
        in_specs=[q_spec, *k_specs, *v_specs, _resident(bias.shape), _resident((2, 1, LANES))],
        out_specs=[out_block(gw), out_block(LANES)],
        out_shape=[jax.ShapeDtypeStruct((bsz, n_tiles, dil, rows, gw), BF16),
                   jax.ShapeDtypeStruct((bsz, n_tiles, dil, rows, LANES), F32)],
        compiler_params=_params(2),
    )(view, view, view, view, view, view, view, bias, headmask)
    return o.reshape(bsz, s, gw), stats.reshape(bsz, s, LANES)


OUT_TOKENS = 2 * TILE


def _attn_out_kernel(h_ref, o0_ref, o1_ref, o2_ref, s0_ref, s1_ref, s2_ref, z_ref, expand_ref,
                     wo_ref, out_ref, slab_ref):
    gw, tm = GROUP_WIDTH, OUT_TOKENS
    n_heads = HEADS_PER_GROUP

    def restride(x, moves, stride):
        n = x.shape[1] // LANES
        for c in range(n):
            for base in range(0, tm, TILE):
                for src, dst, rows in moves:
                    slab_ref[c, pl.ds(base + dst, rows, stride=stride), :] = (
                        x[base + src:base + src + rows, c * LANES:(c + 1) * LANES])
        return jnp.concatenate([slab_ref[c] for c in range(n)], axis=1)

    def to_token_order(x, dil):
        step = UNSPLIT_ROW_STRIDE
        if dil > step:
            assert dil == step * step
            rows, block = TILE // dil, TILE // step
            x = restride(x, [((r1 + step * r2) * rows, r1 * block + r2, rows)
                             for r1 in range(step) for r2 in range(dil // step)], dil // step)
        if dil > 1:
            rows = TILE // step
            x = restride(x, [(r * rows, r, rows) for r in range(step)], step)
        return x

    dils = [dil for _, dil in DIL_PAIRS]
    ms = [to_token_order(ref[0], dil) for ref, dil in zip((s0_ref, s1_ref, s2_ref), dils)]
    ls = [pltpu.roll(m, LANES - n_heads, axis=1) for m in ms]
    top = jnp.maximum(jnp.maximum(ms[0], ms[1]), ms[2])
    ws = [jnp.exp2(m - top) for m in ms]
    denom = ls[0] * ws[0] + ls[1] * ws[1] + ls[2] * ws[2]
    lane = lax.broadcasted_iota(jnp.int32, (tm, LANES), 1)
    denom = jnp.where(lane < n_heads, denom, 1.0)
    acc = h_ref[0]
    for g, o_ref in enumerate((o0_ref, o1_ref, o2_ref)):
        scale = ws[g] / denom
        hi = scale.astype(BF16)
        lo = (scale - hi.astype(F32)).astype(BF16)
        spread = jnp.dot(jnp.concatenate([hi, lo], axis=1), expand_ref[...],
                         preferred_element_type=F32)
        z = z_ref[0, :, g * gw:(g + 1) * gw].astype(F32)
        y = to_token_order(o_ref[0].astype(F32), dils[g]) * spread * _silu(z)
        acc = acc + jnp.dot(y.astype(BF16), wo_ref[g], preferred_element_type=F32)
    out_ref[0] = acc


def _attn_out(h, outs, stats, z, w_out):
    bsz, s, d = h.shape
    gw, tm = GROUP_WIDTH, OUT_TOKENS
    assert s % tm == 0 and tm % TILE == 0 and w_out.shape == (N_GROUPS * gw, d)
    expand = jnp.asarray(
        np.arange(2 * LANES)[:, None] % LANES == np.arange(gw)[None, :] // HEAD_DIM, BF16)
    tok = lambda width: pl.BlockSpec((1, tm, width), lambda b, i: (b, i, 0))
    return pl.pallas_call(
        _attn_out_kernel,
        name="attn_out",
        grid=(bsz, s // tm),
        in_specs=[tok(d)] + [tok(gw)] * 3 + [tok(LANES)] * 3 + [
            tok(N_GROUPS * gw), _resident((2 * LANES, gw)), _resident((N_GROUPS, gw, d))],
        out_specs=tok(d),
        out_shape=jax.ShapeDtypeStruct(h.shape, h.dtype),
        scratch_shapes=[pltpu.VMEM((gw // LANES, tm, LANES), F32)],
        compiler_params=_params(2),
    )(h, *outs, *stats, z, expand, w_out.astype(BF16).reshape(N_GROUPS, gw, d))


def _attn_layer(h, g, w_in, q_gain, k_gain, rel_table, w_out):
    qkv, z = _attn_proj(h, g, w_in, q_gain, k_gain)
    outs, stats = [], []
    for grp, (window, dil) in enumerate(DIL_PAIRS):
        assert (window // 2) // dil == HALF_WINDOW
        o, st = _band_attention(qkv, grp, dil, _band_bias(rel_table, grp, dil))
        outs.append(o)
        stats.append(st)
    return _attn_out(h, outs, stats, z, w_out)


def kernel(x, norm_g, conv_w_in, conv_kernel, conv_bias, conv_w_out, attn_w_in, q_norm_g,
           k_norm_g, attn_w_out, rel_bias_table):
    h = x
    depth = norm_g.shape[0]
    for layer in range(depth):
        j = layer // 2
        if layer % 2 == 0:
            nxt = (attn_w_in[j], attn_w_out[j]) if layer + 1 < depth else ()
            h, attn_weights = _conv_layer(h, norm_g[layer], conv_w_in[j], conv_kernel[j],
                                          conv_bias[j], conv_w_out[j], nxt)
        else:
            w_in, w_out = attn_weights
            h = _attn_layer(h, norm_g[layer], w_in, q_norm_g[j], k_norm_g[j], rel_bias_table,
                            w_out)
    return h
```

```python
import functools

import jax
import jax.numpy as jnp
import numpy as np
from jax import lax
from jax.experimental import pallas as pl
from jax.experimental.pallas import tpu as pltpu

EPS = 1e-6
HEAD_DIM = 64
HEADS_PER_GROUP = 8
GROUP_WIDTH = HEAD_DIM * HEADS_PER_GROUP
DIL_PAIRS = ((128, 1), (512, 4), (2048, 16))
N_GROUPS = len(DIL_PAIRS)
HALF_WINDOW = 64
REL_BUCKETS = 32
REL_MAX_DIST = 1024
MASKED = -1e30
LOG2_E = 1.4426950408889634

LANES = 128
BF16_SUBLANES = 16
MXU_DIM = 256
Q_BLOCK = 128
K_WINDOW = Q_BLOCK + 2 * HALF_WINDOW
TILE = 512
ATTN_TILES = 4
UNSPLIT_ROW_STRIDE = 4
VMEM_LIMIT_BYTES = 56 * 1024 * 1024

BF16 = jnp.bfloat16
F32 = jnp.float32


def _silu(z):
    half = 0.5 * z
    return half + half * jnp.tanh(half)


def _rmsnorm(x, g):
    ms = jnp.mean(x * x, axis=-1, keepdims=True)
    return x * lax.rsqrt(ms + EPS) * g


def _resident(shape):
    zeros = (0,) * len(shape)
    return pl.BlockSpec(shape, lambda *_: zeros, pipeline_mode=pl.Buffered(1))


def _params(n_grid_axes):
    return pltpu.CompilerParams(dimension_semantics=("arbitrary",) * n_grid_axes,
                                vmem_limit_bytes=VMEM_LIMIT_BYTES)


CONV_TOKENS = 1024
CONV_HALO = 8
CONV_CHUNK = 256


def _conv_layer_kernel(xp_ref, x_ref, xn_ref, g_ref, w_ref, cw_ref, cb_ref, wo_ref, *rest):
    n_cast = (len(rest) - 2) // 2
    o_ref, shift_ref = rest[n_cast], rest[-1]
    for src_ref, dst_ref in zip(rest[:n_cast], rest[n_cast + 1:-1]):
        dst_ref[...] = src_ref[...].astype(BF16)
    i = pl.program_id(1)
    last = pl.num_programs(1) - 1
    tm, halo, ce = CONV_TOKENS, CONV_HALO, CONV_CHUNK
    e = wo_ref.shape[0]
    g = g_ref[...]
    x = x_ref[0]
    xp = jnp.where(i > 0, xp_ref[0], 0.0)
    xn = jnp.where(i < last, xn_ref[0], 0.0)
    hn32 = _rmsnorm(x, g)
    hn = hn32.astype(BF16)
    hne = jnp.concatenate([_rmsnorm(xp, g), hn32, _rmsnorm(xn, g)], axis=0).astype(BF16)
    o_ref[0] = x

    for j in range(e // ce):
        ch = slice(j * ce, (j + 1) * ce)

        def proj(lhs, part):
            return jnp.dot(lhs, w_ref[:, part * e + j * ce:part * e + (j + 1) * ce],
                           preferred_element_type=F32)

        p = proj(hne, 1) * proj(hne, 2)
        slabs = shift_ref.at[j % 2]
        for c in range(ce // LANES):
            slabs[c] = p[:, c * LANES:(c + 1) * LANES]

        def shifted(by):
            return jnp.concatenate([slabs[c, halo + by:halo + by + tm, :]
                                    for c in range(ce // LANES)], axis=1)

        conv = (cw_ref[0:1, ch] * shifted(-1) + cw_ref[1:2, ch] * p[halo:halo + tm]
                + cw_ref[2:3, ch] * shifted(1) + cb_ref[:, ch])
        y = proj(hn, 0) * conv * _silu(proj(hn, 3))
        o_ref[0] += jnp.dot(y.astype(BF16), wo_ref[ch, :], preferred_element_type=F32)


def _conv_layer(x, g, w_in, conv_w, conv_b, w_out, to_bf16=()):
    bsz, s, d = x.shape
    e = w_out.shape[0]
    tm, halo, ce = CONV_TOKENS, CONV_HALO, CONV_CHUNK
    assert s % tm == 0 and tm % halo == 0 and e % ce == 0 and w_in.shape == (d, 4 * e)
    per_tile = tm // halo
    n_halo_blocks = s // halo
    n_steps = bsz * (s // tm)
    assert all(a.shape[0] % (n_steps * BF16_SUBLANES) == 0 for a in to_bf16)
    cast_specs = [pl.BlockSpec((a.shape[0] // n_steps, a.shape[1]),
                               lambda b, i: (b * (s // tm) + i, 0)) for a in to_bf16]
    h, *copies = pl.pallas_call(
        _conv_layer_kernel,
        name="conv_layer",
        grid=(bsz, s // tm),
        in_specs=[
            pl.BlockSpec((1, halo, d), lambda b, i: (b, jnp.maximum(i * per_tile - 1, 0), 0)),
            pl.BlockSpec((1, tm, d), lambda b, i: (b, i, 0)),
            pl.BlockSpec((1, halo, d),
                         lambda b, i: (b, jnp.minimum((i + 1) * per_tile, n_halo_blocks - 1), 0)),
            _resident((1, d)),
            _resident((d, 4 * e)),
            _resident((3, e)),
            _resident((1, e)),
            _resident((e, d)),
            *cast_specs,
        ],
        out_specs=[pl.BlockSpec((1, tm, d), lambda b, i: (b, i, 0)), *cast_specs],
        out_shape=[jax.ShapeDtypeStruct(x.shape, x.dtype),
                   *[jax.ShapeDtypeStruct(a.shape, BF16) for a in to_bf16]],
        scratch_shapes=[pltpu.VMEM((2, ce // LANES, tm + 2 * halo, LANES), F32)],
        compiler_params=_params(2),
    )(x, x, x, g.reshape(1, d), w_in.astype(BF16), conv_w, conv_b.reshape(1, e),
      w_out.astype(BF16), *to_bf16)
    return h, copies


def _attn_proj_kernel(h_ref, g_ref, w_ref, gain_ref, headmean_ref, qkv_ref, z_ref, slab_ref,
                      slab2_ref):
    gw, tm = GROUP_WIDTH, TILE
    n_slabs = slab_ref.shape[0]
    n_qkv = 3 * N_GROUPS
    hn32 = _rmsnorm(h_ref[0], g_ref[...])
    hn = hn32.astype(BF16)
    for c in range(n_slabs):
        slab_ref[c] = hn32[:, c * LANES:(c + 1) * LANES]
    step = UNSPLIT_ROW_STRIDE

    def gather_rows(ref, starts, rows, stride):
        return jnp.concatenate(
            [jnp.concatenate([ref[c, pl.ds(start, rows, stride=stride), :] for start in starts],
                             axis=0) for c in range(n_slabs)], axis=1)

    by_step = gather_rows(slab_ref, range(step), tm // step, step)
    for grp, (_, dil) in enumerate(DIL_PAIRS):
        if dil == 1:
            hg = hn
        elif dil == step:
            hg = by_step.astype(BF16)
        else:
            assert dil == step * step
            for c in range(n_slabs):
                slab2_ref[c] = by_step[:, c * LANES:(c + 1) * LANES]
            starts = [(r % step) * (tm // step) + r // step for r in range(dil)]
            hg = gather_rows(slab2_ref, starts, tm // dil, dil // step).astype(BF16)
        for t in range(3):
            c = 3 * grp + t
            y = jnp.dot(hg, w_ref[:, c * gw:(c + 1) * gw], preferred_element_type=F32)
            if t < 2:
                sq = (y * y).astype(BF16)
                ms = jnp.concatenate(
                    [jnp.dot(sq[:, k:k + MXU_DIM], headmean_ref[...], preferred_element_type=F32)
                     for k in range(0, gw, MXU_DIM)], axis=1)
                y = y * lax.rsqrt(ms + EPS) * gain_ref[c]
            qkv_ref[c, 0] = y.astype(BF16)
    for c in range(N_GROUPS):
        z = jnp.dot(hn, w_ref[:, (n_qkv + c) * gw:(n_qkv + c + 1) * gw],
                    preferred_element_type=F32)
        z_ref[0, :, c * gw:(c + 1) * gw] = z.astype(BF16)


def _attn_proj(h, g, w_in, q_gain, k_gain):
    bsz, s, d = h.shape
    gw, tm = GROUP_WIDTH, TILE
    n_qkv = 3 * N_GROUPS
    assert s % tm == 0 and d % LANES == 0 and w_in.shape == (d, (n_qkv + N_GROUPS) * gw)
    gains = jnp.stack([q_gain.reshape(N_GROUPS, gw) * (HEAD_DIM ** -0.5 * LOG2_E),
                       k_gain.reshape(N_GROUPS, gw),
                       jnp.ones((N_GROUPS, gw), F32)], axis=1).reshape(n_qkv, 1, gw)
    head = np.arange(MXU_DIM) // HEAD_DIM
    headmean = jnp.asarray((head[:, None] == head[None, :]) / HEAD_DIM, BF16)
    return pl.pallas_call(
        _attn_proj_kernel,
        name="attn_proj",
        grid=(bsz, s // tm),
        in_specs=[
            pl.BlockSpec((1, tm, d), lambda b, i: (b, i, 0)),
            _resident((1, d)),
            _resident(w_in.shape),
            _resident((n_qkv, 1, gw)),
            _resident((MXU_DIM, MXU_DIM)),
        ],
        out_specs=[
            pl.BlockSpec((n_qkv, 1, tm, gw), lambda b, i: (0, b, i, 0)),
            pl.BlockSpec((1, tm, N_GROUPS * gw), lambda b, i: (b, i, 0)),
        ],
        out_shape=[
            jax.ShapeDtypeStruct((n_qkv, bsz, s, gw), BF16),
            jax.ShapeDtypeStruct((bsz, s, N_GROUPS * gw), BF16),
        ],
        scratch_shapes=[pltpu.VMEM((d // LANES, tm, LANES), F32)] * 2,
        compiler_params=_params(2),
    )(h, g.reshape(1, d), w_in.astype(BF16), gains, headmean)


def _class_row_spans(prev_ref, cur_ref, next_ref, start, stop):
    rows = cur_ref.shape[2]
    seg = cur_ref.shape[0] * rows
    spans, pos = [], start
    while pos < stop:
        if pos < 0:
            ref, base, limit = prev_ref, pos + HALF_WINDOW, 0
        elif pos >= seg:
            ref, base, limit = next_ref, pos - seg, stop
        else:
            ref, base, limit = cur_ref, pos, seg
        tile, first = divmod(base, ref.shape[2])
        n = min(min(stop, limit) - pos, ref.shape[2] - first)
        spans.append((ref, tile, first, n))
        pos += n
    return spans


def _band_attn_kernel(q_ref, kp_ref, kc_ref, kn_ref, vp_ref, vc_ref, vn_ref, bias_ref,
                      headmask_ref, o_ref, stat_ref, *, dil):
    hw, qb = HALF_WINDOW, Q_BLOCK
    n_heads = HEADS_PER_GROUP
    blocks_per_class = q_ref.shape[0] * q_ref.shape[2] // qb
    step, last_step = pl.program_id(1), pl.num_programs(1) - 1
    lane = lax.broadcasted_iota(jnp.int32, (qb, LANES), 1)
    low_half = lane < HEAD_DIM

    def gather(spans, r, cols):
        parts = [ref[tile, r, first:first + n, cols] for ref, tile, first, n in spans]
        return parts[0] if len(parts) == 1 else jnp.concatenate(parts, axis=0)

    def scatter(ref, spans, r, cols, value):
        done = 0
        for _, tile, first, n in spans:
            ref[tile, r, first:first + n, cols] = value[done:done + n]
            done += n

    for r in range(dil):
        for u in range(blocks_per_class):
            q_spans = _class_row_spans(None, q_ref, None, u * qb, (u + 1) * qb)
            k_spans = _class_row_spans(kp_ref, kc_ref, kn_ref, u * qb - hw, (u + 1) * qb + hw)
            v_spans = _class_row_spans(vp_ref, vc_ref, vn_ref, u * qb - hw, (u + 1) * qb + hw)
            variant = 1
            if u == 0:
                variant = jnp.where(step == 0, 0, variant)
            if u == blocks_per_class - 1:
                variant = jnp.where(step == last_step, 2, variant)
            stats = jnp.zeros((qb, LANES), F32)
            for pair in range(n_heads // 2):
                cols = slice(pair * LANES, (pair + 1) * LANES)
                q2 = gather(q_spans, r, cols)
                qq = jnp.concatenate([q2 * headmask_ref[0], q2 * headmask_ref[1]], axis=0)
                kw = gather(k_spans, r, cols)
                vw = jnp.concatenate([gather(v_spans, r, cols),
                                      jnp.ones((K_WINDOW, LANES), BF16)], axis=1)
                s = lax.dot_general(qq, kw, (((1,), (1,)), ((), ())),
                                    preferred_element_type=F32)
                s = s + bias_ref[variant, pair]
                m = jnp.max(s, axis=-1, keepdims=True)
                e = jnp.exp2(s - m)
                o2 = jnp.dot(e.astype(BF16), vw, preferred_element_type=F32)
                l = o2[:, LANES:]
                o = jnp.where(low_half, o2[:qb, :LANES], o2[qb:, :LANES]).astype(BF16)
                scatter(o_ref, q_spans, r, cols, o)
                for k in range(2):
                    head = 2 * pair + k
                    stats = jnp.where(lane == head, m[k * qb:(k + 1) * qb], stats)
                    stats = jnp.where(lane == n_heads + head, l[k * qb:(k + 1) * qb], stats)
            scatter(stat_ref, q_spans, r, slice(None), stats)


def _t5_bucket(rel):
    nb = REL_BUCKETS // 2
    ret = (rel > 0).astype(np.int32) * nb
    n = np.abs(rel)
    max_exact = nb // 2
    large = max_exact + (np.log(np.maximum(n, 1) / max_exact)
                         / np.log(REL_MAX_DIST / max_exact) * (nb - max_exact)).astype(np.int32)
    large = np.minimum(large, nb - 1)
    return ret + np.where(n < max_exact, n, large).astype(np.int32)


def _band_bias(rel_table, group, dil):
    hw, qb, kw = HALF_WINDOW, Q_BLOCK, K_WINDOW
    n_heads = HEADS_PER_GROUP
    buckets = _t5_bucket(np.arange(-hw, hw + 1) * dil)
    heads = slice(group * n_heads, (group + 1) * n_heads)
    per_offset = rel_table[buckets][:, heads].astype(F32).T * LOG2_E
    width = qb + kw
    pad = jnp.full((n_heads, qb - 1), MASKED, F32)
    vec = jnp.concatenate([pad, per_offset, pad, jnp.full((n_heads, 2), MASKED, F32)], axis=1)
    assert vec.shape[1] == width + 1
    skew = jnp.tile(vec, (1, qb))[:, :qb * width].reshape(n_heads, qb, width)
    mid = skew[:, :, qb - 1:qb - 1 + kw]
    kj = np.arange(kw)[None, None, :]
    first = jnp.where(kj >= hw, mid, MASKED)
    final = jnp.where(kj < hw + qb, mid, MASKED)
    return jnp.stack([first, mid, final]).reshape(3, n_heads // 2, 2 * qb, kw)


def _band_attention(qkv, group, dil, bias):
    n_qkv, bsz, s, gw = qkv.shape
    hw, qb, tiles = HALF_WINDOW, Q_BLOCK, ATTN_TILES
    rows = TILE // dil
    n_tiles = s // TILE
    n_steps = n_tiles // tiles
    assert s % (TILE * tiles) == 0 and (tiles * rows) % qb == 0 and n_steps >= 2
    assert rows % hw == 0 or hw % rows == 0
    view = qkv.reshape(n_qkv, bsz, n_tiles, dil, rows, gw)
    headmask = jnp.asarray(
        (np.arange(LANES)[None, :] // HEAD_DIM == np.arange(2)[:, None])[:, None, :], BF16)
    if rows >= hw:
        halo_block = (None, None, 1, dil, hw, gw)
        sub = rows // hw
        prev_idx = lambda i: (jnp.maximum(tiles * i - 1, 0), 0, sub - 1, 0)
        next_idx = lambda i: (jnp.minimum(tiles * (i + 1), n_tiles - 1), 0, 0, 0)
    else:
        per = hw // rows
        halo_block = (None, None, per, dil, rows, gw)
        prev_idx = lambda i: (jnp.maximum(tiles // per * i - 1, 0), 0, 0, 0)
        next_idx = lambda i: (jnp.minimum(tiles // per * (i + 1), n_tiles // per - 1), 0, 0, 0)

    def specs(c):
        cur = pl.BlockSpec((None, None, tiles, dil, rows, gw), lambda b, i: (c, b, i, 0, 0, 0))
        prev = pl.BlockSpec(halo_block, lambda b, i: (c, b) + prev_idx(i))
        nxt = pl.BlockSpec(halo_block, lambda b, i: (c, b) + next_idx(i))
        return prev, cur, nxt

    (_, q_spec, _), k_specs, v_specs = specs(3 * group), specs(3 * group + 1), specs(3 * group + 2)
    out_block = lambda width: pl.BlockSpec((None, tiles, dil, rows, width),
                                           lambda b, i: (b, i, 0, 0, 0))
    o, stats = pl.pallas_call(
        functools.partial(_band_attn_kernel, dil=dil),
        name=f"band_attn_d{dil}",
        grid=(bsz, n_steps),
        in_specs=[q_spec, *k_specs, *v_specs, _resident(bias.shape), _resident((2, 1, LANES))],
        out_specs=[out_block(gw), out_block(LANES)],
        out_shape=[jax.ShapeDtypeStruct((bsz, n_tiles, dil, rows, gw), BF16),
                   jax.ShapeDtypeStruct((bsz, n_tiles, dil, rows, LANES), F32)],
        compiler_params=_params(2),
    )(view, view, view, view, view, view, view, bias, headmask)
    return o.reshape(bsz, s, gw), stats.reshape(bsz, s, LANES)


OUT_TOKENS = 2 * TILE


def _attn_out_kernel(h_ref, o0_ref, o1_ref, o2_ref, s0_ref, s1_ref, s2_ref, z_ref, expand_ref,
                     wo_ref, out_ref, slab_ref):
    gw, tm = GROUP_WIDTH, OUT_TOKENS
    n_heads = HEADS_PER_GROUP

    def restride(x, moves, stride):
        n = x.shape[1] // LANES
        for c in range(n):
            for base in range(0, tm, TILE):
                for src, dst, rows in moves:
                    slab_ref[c, pl.ds(base + dst, rows, stride=stride), :] = (
                        x[base + src:base + src + rows, c * LANES:(c + 1) * LANES])
        return jnp.concatenate([slab_ref[c] for c in range(n)], axis=1)

    def to_token_order(x, dil):
        step = UNSPLIT_ROW_STRIDE
        if dil > step:
            assert dil == step * step
            rows, block = TILE // dil, TILE // step
            x = restride(x, [((r1 + step * r2) * rows, r1 * block + r2, rows)
                             for r1 in range(step) for r2 in range(dil // step)], dil // step)
        if dil > 1:
            rows = TILE // step
            x = restride(x, [(r * rows, r, rows) for r in range(step)], step)
        return x

    dils = [dil for _, dil in DIL_PAIRS]
    ms = [to_token_order(ref[0], dil) for ref, dil in zip((s0_ref, s1_ref, s2_ref), dils)]
    ls = [pltpu.roll(m, LANES - n_heads, axis=1) for m in ms]
    top = jnp.maximum(jnp.maximum(ms[0], ms[1]), ms[2])
    ws = [jnp.exp2(m - top) for m in ms]
    denom = ls[0] * ws[0] + ls[1] * ws[1] + ls[2] * ws[2]
    lane = lax.broadcasted_iota(jnp.int32, (tm, LANES), 1)
    denom = jnp.where(lane < n_heads, denom, 1.0)
    acc = h_ref[0]
    for g, o_ref in enumerate((o0_ref, o1_ref, o2_ref)):
        scale = ws[g] / denom
        hi = scale.astype(BF16)
        lo = (scale - hi.astype(F32)).astype(BF16)
        spread = jnp.dot(jnp.concatenate([hi, lo], axis=1), expand_ref[...],
                         preferred_element_type=F32)
        z = z_ref[0, :, g * gw:(g + 1) * gw].astype(F32)
        y = to_token_order(o_ref[0].astype(F32), dils[g]) * spread * _silu(z)
        acc = acc + jnp.dot(y.astype(BF16), wo_ref[g], preferred_element_type=F32)
    out_ref[0] = acc


def _attn_out(h, outs, stats, z, w_out):
    bsz, s, d = h.shape
    gw, tm = GROUP_WIDTH, OUT_TOKENS
    assert s % tm == 0 and tm % TILE == 0 and w_out.shape == (N_GROUPS * gw, d)
    expand = jnp.asarray(
        np.arange(2 * LANES)[:, None] % LANES == np.arange(gw)[None, :] // HEAD_DIM, BF16)
    tok = lambda width: pl.BlockSpec((1, tm, width), lambda b, i: (b, i, 0))
    return pl.pallas_call(
        _attn_out_kernel,
        name="attn_out",
        grid=(bsz, s // tm),
        in_specs=[tok(d)] + [tok(gw)] * 3 + [tok(LANES)] * 3 + [
            tok(N_GROUPS * gw), _resident((2 * LANES, gw)), _resident((N_GROUPS, gw, d))],
        out_specs=tok(d),
        out_shape=jax.ShapeDtypeStruct(h.shape, h.dtype),
        scratch_shapes=[pltpu.VMEM((gw // LANES, tm, LANES), F32)],
        compiler_params=_params(2),
    )(h, *outs, *stats, z, expand, w_out.astype(BF16).reshape(N_GROUPS, gw, d))


def _attn_layer(h, g, w_in, q_gain, k_gain, rel_table, w_out):
    qkv, z = _attn_proj(h, g, w_in, q_gain, k_gain)
    outs, stats = [], []
    for grp, (window, dil) in enumerate(DIL_PAIRS):
        assert (window // 2) // dil == HALF_WINDOW
        o, st = _band_attention(qkv, grp, dil, _band_bias(rel_table, grp, dil))
        outs.append(o)
        stats.append(st)
    return _attn_out(h, outs, stats, z, w_out)


def kernel(x, norm_g, conv_w_in, conv_kernel, conv_bias, conv_w_out, attn_w_in, q_norm_g,
           k_norm_g, attn_w_out, rel_bias_table):
    h = x
    depth = norm_g.shape[0]
    for layer in range(depth):
        j = layer // 2
        if layer % 2 == 0:
            nxt = (attn_w_in[j], attn_w_out[j]) if layer + 1 < depth else ()
            h, attn_weights = _conv_layer(h, norm_g[layer], conv_w_in[j], conv_kernel[j],
                                          conv_bias[j], conv_w_out[j], nxt)
        else:
            w_in, w_out = attn_weights
            h = _attn_layer(h, norm_g[layer], w_in, q_norm_g[j], k_norm_g[j], rel_bias_table,
                            w_out)
    return h
```

```python
import functools

import jax
import jax.numpy as jnp
import numpy as np
from jax import lax
from jax.experimental import pallas as pl
from jax.experimental.pallas import tpu as pltpu

EPS = 1e-6
HEAD_DIM = 64
HEADS_PER_GROUP = 8
GROUP_WIDTH = HEAD_DIM * HEADS_PER_GROUP
DIL_PAIRS = ((128, 1), (512, 4), (2048, 16))
N_GROUPS = len(DIL_PAIRS)
HALF_WINDOW = 64
REL_BUCKETS = 32
REL_MAX_DIST = 1024
MASKED = -1e30
LOG2_E = 1.4426950408889634

LANES = 128
BF16_SUBLANES = 16
MXU_DIM = 256
Q_BLOCK = 128
K_WINDOW = Q_BLOCK + 2 * HALF_WINDOW
TILE = 512
ATTN_TILES = 4
UNSPLIT_ROW_STRIDE = 4
VMEM_LIMIT_BYTES = 56 * 1024 * 1024

BF16 = jnp.bfloat16
F32 = jnp.float32


def _silu(z):
    half = 0.5 * z
    return half + half * jnp.tanh(half)


def _rmsnorm(x, g):
    ms = jnp.mean(x * x, axis=-1, keepdims=True)
    return x * lax.rsqrt(ms + EPS) * g


def _resident(shape):
    zeros = (0,) * len(shape)
    return pl.BlockSpec(shape, lambda *_: zeros, pipeline_mode=pl.Buffered(1))


def _params(n_grid_axes):
    return pltpu.CompilerParams(dimension_semantics=("arbitrary",) * n_grid_axes,
                                vmem_limit_bytes=VMEM_LIMIT_BYTES)


CONV_TOKENS = 1024
CONV_HALO = 8
CONV_CHUNK = 256


def _conv_layer_kernel(xp_ref, x_ref, xn_ref, g_ref, w_ref, cw_ref, cb_ref, wo_ref, *rest):
    n_cast = (len(rest) - 2) // 2
    o_ref, shift_ref = rest[n_cast], rest[-1]
    for src_ref, dst_ref in zip(rest[:n_cast], rest[n_cast + 1:-1]):
        dst_ref[...] = src_ref[...].astype(BF16)
    i = pl.program_id(1)
    last = pl.num_programs(1) - 1
    tm, halo, ce = CONV_TOKENS, CONV_HALO, CONV_CHUNK
    e = wo_ref.shape[0]
    g = g_ref[...]
    x = x_ref[0]
    xp = jnp.where(i > 0, xp_ref[0], 0.0)
    xn = jnp.where(i < last, xn_ref[0], 0.0)
    hn32 = _rmsnorm(x, g)
    hn = hn32.astype(BF16)
    hne = jnp.concatenate([_rmsnorm(xp, g), hn32, _rmsnorm(xn, g)], axis=0).astype(BF16)
    o_ref[0] = x

    for j in range(e // ce):
        ch = slice(j * ce, (j + 1) * ce)

        def proj(lhs, part):
            return jnp.dot(lhs, w_ref[:, part * e + j * ce:part * e + (j + 1) * ce],
                           preferred_element_type=F32)

        p = proj(hne, 1) * proj(hne, 2)
        slabs = shift_ref.at[j % 2]
        for c in range(ce // LANES):
            slabs[c] = p[:, c * LANES:(c + 1) * LANES]

        def shifted(by):
            return jnp.concatenate([slabs[c, halo + by:halo + by + tm, :]
                                    for c in range(ce // LANES)], axis=1)

        conv = (cw_ref[0:1, ch] * shifted(-1) + cw_ref[1:2, ch] * p[halo:halo + tm]
                + cw_ref[2:3, ch] * shifted(1) + cb_ref[:, ch])
        y = proj(hn, 0) * conv * _silu(proj(hn, 3))
        o_ref[0] += jnp.dot(y.astype(BF16), wo_ref[ch, :], preferred_element_type=F32)


def _conv_layer(x, g, w_in, conv_w, conv_b, w_out, to_bf16=()):
    bsz, s, d = x.shape
    e = w_out.shape[0]
    tm, halo, ce = CONV_TOKENS, CONV_HALO, CONV_CHUNK
    assert s % tm == 0 and tm % halo == 0 and e % ce == 0 and w_in.shape == (d, 4 * e)
    per_tile = tm // halo
    n_halo_blocks = s // halo
    n_steps = bsz * (s // tm)
    assert all(a.shape[0] % (n_steps * BF16_SUBLANES) == 0 for a in to_bf16)
    cast_specs = [pl.BlockSpec((a.shape[0] // n_steps, a.shape[1]),
                               lambda b, i: (b * (s // tm) + i, 0)) for a in to_bf16]
    h, *copies = pl.pallas_call(
        _conv_layer_kernel,
        name="conv_layer",
        grid=(bsz, s // tm),
        in_specs=[
            pl.BlockSpec((1, halo, d), lambda b, i: (b, jnp.maximum(i * per_tile - 1, 0), 0)),
            pl.BlockSpec((1, tm, d), lambda b, i: (b, i, 0)),
            pl.BlockSpec((1, halo, d),
                         lambda b, i: (b, jnp.minimum((i + 1) * per_tile, n_halo_blocks - 1), 0)),
            _resident((1, d)),
            _resident((d, 4 * e)),
            _resident((3, e)),
            _resident((1, e)),
            _resident((e, d)),
            *cast_specs,
        ],
        out_specs=[pl.BlockSpec((1, tm, d), lambda b, i: (b, i, 0)), *cast_specs],
        out_shape=[jax.ShapeDtypeStruct(x.shape, x.dtype),
                   *[jax.ShapeDtypeStruct(a.shape, BF16) for a in to_bf16]],
        scratch_shapes=[pltpu.VMEM((2, ce // LANES, tm + 2 * halo, LANES), F32)],
        compiler_params=_params(2),
    )(x, x, x, g.reshape(1, d), w_in.astype(BF16), conv_w, conv_b.reshape(1, e),
      w_out.astype(BF16), *to_bf16)
    return h, copies


def _attn_proj_kernel(h_ref, g_ref, w_ref, gain_ref, headmean_ref, qkv_ref, z_ref, slab_ref,
                      slab2_ref):
    gw, tm = GROUP_WIDTH, TILE
    n_slabs = slab_ref.shape[0]
    n_qkv = 3 * N_GROUPS
    hn32 = _rmsnorm(h_ref[0], g_ref[...])
    hn = hn32.astype(BF16)
    for c in range(n_slabs):
        slab_ref[c] = hn32[:, c * LANES:(c + 1) * LANES]
    step = UNSPLIT_ROW_STRIDE

    def gather_rows(ref, starts, rows, stride):
        return jnp.concatenate(
            [jnp.concatenate([ref[c, pl.ds(start, rows, stride=stride), :] for start in starts],
                             axis=0) for c in range(n_slabs)], axis=1)

    by_step = gather_rows(slab_ref, range(step), tm // step, step)
    for c in range(N_GROUPS):
        z = jnp.dot(hn, w_ref[:, (n_qkv + c) * gw:(n_qkv + c + 1) * gw],
                    preferred_element_type=F32)
        z_ref[0, :, c * gw:(c + 1) * gw] = _silu(z).astype(BF16)
    for grp, (_, dil) in enumerate(DIL_PAIRS):
        if dil == 1:
            hg = hn
        elif dil == step:
            hg = by_step.astype(BF16)
        else:
            assert dil == step * step
            for c in range(n_slabs):
                slab2_ref[c] = by_step[:, c * LANES:(c + 1) * LANES]
            starts = [(r % step) * (tm // step) + r // step for r in range(dil)]
            hg = gather_rows(slab2_ref, starts, tm // dil, dil // step).astype(BF16)
        for t in range(3):
            c = 3 * grp + t
            y = jnp.dot(hg, w_ref[:, c * gw:(c + 1) * gw], preferred_element_type=F32)
            if t < 2:
                sq = (y * y).astype(BF16)
                ms = jnp.concatenate(
                    [jnp.dot(sq[:, k:k + MXU_DIM], headmean_ref[...], preferred_element_type=F32)
                     for k in range(0, gw, MXU_DIM)], axis=1)
                y = y * lax.rsqrt(ms + EPS) * gain_ref[c]
            qkv_ref[c, 0] = y.astype(BF16)


def _attn_proj(h, g, w_in, q_gain, k_gain):
    bsz, s, d = h.shape
    gw, tm = GROUP_WIDTH, TILE
    n_qkv = 3 * N_GROUPS
    assert s % tm == 0 and d % LANES == 0 and w_in.shape == (d, (n_qkv + N_GROUPS) * gw)
    gains = jnp.stack([q_gain.reshape(N_GROUPS, gw) * (HEAD_DIM ** -0.5 * LOG2_E),
                       k_gain.reshape(N_GROUPS, gw),
                       jnp.ones((N_GROUPS, gw), F32)], axis=1).reshape(n_qkv, 1, gw)
    head = np.arange(MXU_DIM) // HEAD_DIM
    headmean = jnp.asarray((head[:, None] == head[None, :]) / HEAD_DIM, BF16)
    return pl.pallas_call(
        _attn_proj_kernel,
        name="attn_proj",
        grid=(bsz, s // tm),
        in_specs=[
            pl.BlockSpec((1, tm, d), lambda b, i: (b, i, 0)),
            _resident((1, d)),
            _resident(w_in.shape),
            _resident((n_qkv, 1, gw)),
            _resident((MXU_DIM, MXU_DIM)),
        ],
        out_specs=[
            pl.BlockSpec((n_qkv, 1, tm, gw), lambda b, i: (0, b, i, 0)),
            pl.BlockSpec((1, tm, N_GROUPS * gw), lambda b, i: (b, i, 0)),
        ],
        out_shape=[
            jax.ShapeDtypeStruct((n_qkv, bsz, s, gw), BF16),
            jax.ShapeDtypeStruct((bsz, s, N_GROUPS * gw), BF16),
        ],
        scratch_shapes=[pltpu.VMEM((d // LANES, tm, LANES), F32)] * 2,
        compiler_params=_params(2),
    )(h, g.reshape(1, d), w_in.astype(BF16), gains, headmean)


def _class_row_spans(prev_ref, cur_ref, next_ref, start, stop):
    rows = cur_ref.shape[2]
    seg = cur_ref.shape[0] * rows
    spans, pos = [], start
    while pos < stop:
        if pos < 0:
            ref, base, limit = prev_ref, pos + HALF_WINDOW, 0
        elif pos >= seg:
            ref, base, limit = next_ref, pos - seg, stop
        else:
            ref, base, limit = cur_ref, pos, seg
        tile, first = divmod(base, ref.shape[2])
        n = min(min(stop, limit) - pos, ref.shape[2] - first)
        spans.append((ref, tile, first, n))
        pos += n
    return spans


def _band_attn_kernel(q_ref, kp_ref, kc_ref, kn_ref, vp_ref, vc_ref, vn_ref, bias_ref,
                      headmask_ref, o_ref, stat_ref, *, dil):
    hw, qb = HALF_WINDOW, Q_BLOCK
    n_heads = HEADS_PER_GROUP
    blocks_per_class = q_ref.shape[0] * q_ref.shape[2] // qb
    step, last_step = pl.program_id(1), pl.num_programs(1) - 1
    lane = lax.broadcasted_iota(jnp.int32, (qb, LANES), 1)
    low_half = lane < HEAD_DIM

    def gather(spans, r, cols):
        parts = [ref[tile, r, first:first + n, cols] for ref, tile, first, n in spans]
        return parts[0] if len(parts) == 1 else jnp.concatenate(parts, axis=0)

    def scatter(ref, spans, r, cols, value):
        done = 0
        for _, tile, first, n in spans:
            ref[tile, r, first:first + n, cols] = value[done:done + n]
            done += n

    for r in range(dil):
        for u in range(blocks_per_class):
            q_spans = _class_row_spans(None, q_ref, None, u * qb, (u + 1) * qb)
            k_spans = _class_row_spans(kp_ref, kc_ref, kn_ref, u * qb - hw, (u + 1) * qb + hw)
            v_spans = _class_row_spans(vp_ref, vc_ref, vn_ref, u * qb - hw, (u + 1) * qb + hw)
            variant = 1
            if u == 0:
                variant = jnp.where(step == 0, 0, variant)
            if u == blocks_per_class - 1:
                variant = jnp.where(step == last_step, 2, variant)
            stats = jnp.zeros((qb, LANES), F32)
            for pair in range(n_heads // 2):
                cols = slice(pair * LANES, (pair + 1) * LANES)
                q2 = gather(q_spans, r, cols)
                qq = jnp.concatenate([q2 * headmask_ref[0], q2 * headmask_ref[1]], axis=0)
                kw = gather(k_spans, r, cols)
                vw = jnp.concatenate([gather(v_spans, r, cols),
                                      jnp.ones((K_WINDOW, LANES), BF16)], axis=1)
                s = lax.dot_general(qq, kw, (((1,), (1,)), ((), ())),
                                    preferred_element_type=F32)
                s = s + bias_ref[variant, pair]
                m = jnp.max(s, axis=-1, keepdims=True)
                e = jnp.exp2(s - m)
                o2 = jnp.dot(e.astype(BF16), vw, preferred_element_type=F32)
                l = o2[:, LANES:]
                o = jnp.where(low_half, o2[:qb, :LANES], o2[qb:, :LANES]).astype(BF16)
                scatter(o_ref, q_spans, r, cols, o)
                for k in range(2):
                    head = 2 * pair + k
                    stats = jnp.where(lane == head, m[k * qb:(k + 1) * qb], stats)
                    stats = jnp.where(lane == n_heads + head, l[k * qb:(k + 1) * qb], stats)
            scatter(stat_ref, q_spans, r, slice(None), stats)


def _t5_bucket(rel):
    nb = REL_BUCKETS // 2
    ret = (rel > 0).astype(np.int32) * nb
    n = np.abs(rel)
    max_exact = nb // 2
    large = max_exact + (np.log(np.maximum(n, 1) / max_exact)
                         / np.log(REL_MAX_DIST / max_exact) * (nb - max_exact)).astype(np.int32)
    large = np.minimum(large, nb - 1)
    return ret + np.where(n < max_exact, n, large).astype(np.int32)


def _band_bias(rel_table, group, dil):
    hw, qb, kw = HALF_WINDOW, Q_BLOCK, K_WINDOW
    n_heads = HEADS_PER_GROUP
    buckets = _t5_bucket(np.arange(-hw, hw + 1) * dil)
    heads = slice(group * n_heads, (group + 1) * n_heads)
    per_offset = rel_table[buckets][:, heads].astype(F32).T * LOG2_E
    width = qb + kw
    pad = jnp.full((n_heads, qb - 1), MASKED, F32)
    vec = jnp.concatenate([pad, per_offset, pad, jnp.full((n_heads, 2), MASKED, F32)], axis=1)
    assert vec.shape[1] == width + 1
    skew = jnp.tile(vec, (1, qb))[:, :qb * width].reshape(n_heads, qb, width)
    mid = skew[:, :, qb - 1:qb - 1 + kw]
    kj = np.arange(kw)[None, None, :]
    first = jnp.where(kj >= hw, mid, MASKED)
    final = jnp.where(kj < hw + qb, mid, MASKED)
    return jnp.stack([first, mid, final]).reshape(3, n_heads // 2, 2 * qb, kw)


def _band_attention(qkv, group, dil, bias):
    n_qkv, bsz, s, gw = qkv.shape
    hw, qb, tiles = HALF_WINDOW, Q_BLOCK, ATTN_TILES
    rows = TILE // dil
    n_tiles = s // TILE
    n_steps = n_tiles // tiles
    assert s % (TILE * tiles) == 0 and (tiles * rows) % qb == 0 and n_steps >= 2
    assert rows % hw == 0 or hw % rows == 0
    view = qkv.reshape(n_qkv, bsz, n_tiles, dil, rows, gw)
    headmask = jnp.asarray(
        (np.arange(LANES)[None, :] // HEAD_DIM == np.arange(2)[:, None])[:, None, :], BF16)
    if rows >= hw:
        halo_block = (None, None, 1, dil, hw, gw)
        sub = rows // hw
        prev_idx = lambda i: (jnp.maximum(tiles * i - 1, 0), 0, sub - 1, 0)
        next_idx = lambda i: (jnp.minimum(tiles * (i + 1), n_tiles - 1), 0, 0, 0)
    else:
        per = hw // rows
        halo_block = (None, None, per, dil, rows, gw)
        prev_idx = lambda i: (jnp.maximum(tiles // per * i - 1, 0), 0, 0, 0)
        next_idx = lambda i: (jnp.minimum(tiles // per * (i + 1), n_tiles // per - 1), 0, 0, 0)

    def specs(c):
        cur = pl.BlockSpec((None, None, tiles, dil, rows, gw), lambda b, i: (c, b, i, 0, 0, 0))
        prev = pl.BlockSpec(halo_block, lambda b, i: (c, b) + prev_idx(i))
        nxt = pl.BlockSpec(halo_block, lambda b, i: (c, b) + next_idx(i))
        return prev, cur, nxt

    (_, q_spec, _), k_specs, v_specs = specs(3 * group), specs(3 * group + 1), specs(3 * group + 2)
    out_block = lambda width: pl.BlockSpec((None, tiles, dil, rows, width),
                                           lambda b, i: (b, i, 0, 0, 0))
    o, stats = pl.pallas_call(
        functools.partial(_band_attn_kernel, dil=dil),
        name=f"band_attn_d{dil}",
        grid=(bsz, n_steps),
        in_specs=[q_spec, *k_specs, *v_specs, _resident(bias.shape), _resident((2, 1, LANES))],
        out_specs=[out_block(gw), out_block(LANES)],
        out_shape=[jax.ShapeDtypeStruct((bsz, n_tiles, dil, rows, gw), BF16),
                   jax.ShapeDtypeStruct((bsz, n_tiles, dil, rows, LANES), F32)],
        compiler_params=_params(2),
    )(view, view, view, view, view, view, view, bias, headmask)
    return o.reshape(bsz, s, gw), stats.reshape(bsz, s, LANES)


OUT_TOKENS = 2 * TILE


def _attn_out_kernel(h_ref, o0_ref, o1_ref, o2_ref, s0_ref, s1_ref, s2_ref, gate_ref, expand_ref,
                     wo_ref, out_ref, slab_ref):
    gw, tm = GROUP_WIDTH, OUT_TOKENS
    n_heads = HEADS_PER_GROUP

    def restride(x, moves, stride):
        n = x.shape[1] // LANES
        for c in range(n):
            for base in range(0, tm, TILE):
                for src, dst, rows in moves:
                    slab_ref[c, pl.ds(base + dst, rows, stride=stride), :] = (
                        x[base + src:base + src + rows, c * LANES:(c + 1) * LANES])
        return jnp.concatenate([slab_ref[c] for c in range(n)], axis=1)

    def to_token_order(x, dil):
        step = UNSPLIT_ROW_STRIDE
        if dil > step:
            assert dil == step * step
            rows, block = TILE // dil, TILE // step
            x = restride(x, [((r1 + step * r2) * rows, r1 * block + r2, rows)
                             for r1 in range(step) for r2 in range(dil // step)], dil // step)
        if dil > 1:
            rows = TILE // step
            x = restride(x, [(r * rows, r, rows) for r in range(step)], step)
        return x

    dils = [dil for _, dil in DIL_PAIRS]
    ms = [to_token_order(ref[0], dil) for ref, dil in zip((s0_ref, s1_ref, s2_ref), dils)]
    ls = [pltpu.roll(m, LANES - n_heads, axis=1) for m in ms]
    top = jnp.maximum(jnp.maximum(ms[0], ms[1]), ms[2])
    ws = [jnp.exp2(m - top) for m in ms]
    denom = ls[0] * ws[0] + ls[1] * ws[1] + ls[2] * ws[2]
    lane = lax.broadcasted_iota(jnp.int32, (tm, LANES), 1)
    denom = jnp.where(lane < n_heads, denom, 1.0)
    acc = h_ref[0]
    for g, o_ref in enumerate((o0_ref, o1_ref, o2_ref)):
        scale = ws[g] / denom
        hi = scale.astype(BF16)
        lo = (scale - hi.astype(F32)).astype(BF16)
        spread = jnp.dot(jnp.concatenate([hi, lo], axis=1), expand_ref[...],
                         preferred_element_type=F32)
        gate = gate_ref[0, :, g * gw:(g + 1) * gw].astype(F32)
        y = to_token_order(o_ref[0].astype(F32), dils[g]) * spread * gate
        acc = acc + jnp.dot(y.astype(BF16), wo_ref[g], preferred_element_type=F32)
    out_ref[0] = acc


def _attn_out(h, outs, stats, z, w_out):
    bsz, s, d = h.shape
    gw, tm = GROUP_WIDTH, OUT_TOKENS
    assert s % tm == 0 and tm % TILE == 0 and w_out.shape == (N_GROUPS * gw, d)
    expand = jnp.asarray(
        np.arange(2 * LANES)[:, None] % LANES == np.arange(gw)[None, :] // HEAD_DIM, BF16)
    tok = lambda width: pl.BlockSpec((1, tm, width), lambda b, i: (b, i, 0))
    return pl.pallas_call(
        _attn_out_kernel,
        name="attn_out",
        grid=(bsz, s // tm),
        in_specs=[tok(d)] + [tok(gw)] * 3 + [tok(LANES)] * 3 + [
            tok(N_GROUPS * gw), _resident((2 * LANES, gw)), _resident((N_GROUPS, gw, d))],
        out_specs=tok(d),
        out_shape=jax.ShapeDtypeStruct(h.shape, h.dtype),
        scratch_shapes=[pltpu.VMEM((gw // LANES, tm, LANES), F32)],
        compiler_params=_params(2),
    )(h, *outs, *stats, z, expand, w_out.astype(BF16).reshape(N_GROUPS, gw, d))


def _attn_layer(h, g, w_in, q_gain, k_gain, rel_table, w_out):
    qkv, z = _attn_proj(h, g, w_in, q_gain, k_gain)
    outs, stats = [], []
    for grp, (window, dil) in enumerate(DIL_PAIRS):
        assert (window // 2) // dil == HALF_WINDOW
        o, st = _band_attention(qkv, grp, dil, _band_bias(rel_table, grp, dil))
        outs.append(o)
        stats.append(st)
    return _attn_out(h, outs, stats, z, w_out)


def kernel(x, norm_g, conv_w_in, conv_kernel, conv_bias, conv_w_out, attn_w_in, q_norm_g,
           k_norm_g, attn_w_out, rel_bias_table):
    h = x
    depth = norm_g.shape[0]
    for layer in range(depth):
        j = layer // 2
        if layer % 2 == 0:
            nxt = (attn_w_in[j], attn_w_out[j]) if layer + 1 < depth else ()
            h, attn_weights = _conv_layer(h, norm_g[layer], conv_w_in[j], conv_kernel[j],
                                          conv_bias[j], conv_w_out[j], nxt)
        else:
            w_in, w_out = attn_weights
            h = _attn_layer(h, norm_g[layer], w_in, q_norm_g[j], k_norm_g[j], rel_bias_table,
                            w_out)
    return h
```

```python
import functools

import jax
import jax.numpy as jnp
import numpy as np
from jax import lax
from jax.experimental import pallas as pl
from jax.experimental.pallas import tpu as pltpu

EPS = 1e-6
HEAD_DIM = 64
HEADS_PER_GROUP = 8
GROUP_WIDTH = HEAD_DIM * HEADS_PER_GROUP
DIL_PAIRS = ((128, 1), (512, 4), (2048, 16))
N_GROUPS = len(DIL_PAIRS)
HALF_WINDOW = 64
REL_BUCKETS = 32
REL_MAX_DIST = 1024
MASKED = -1e30
LOG2_E = 1.4426950408889634

LANES = 128
BF16_SUBLANES = 16
MXU_DIM = 256
Q_BLOCK = 128
K_WINDOW = Q_BLOCK + 2 * HALF_WINDOW
TILE = 512
ATTN_TILES = 4
UNSPLIT_ROW_STRIDE = 4
VMEM_LIMIT_BYTES = 56 * 1024 * 1024

BF16 = jnp.bfloat16
F32 = jnp.float32


def _silu(z):
    half = 0.5 * z
    return half + half * jnp.tanh(half)


def _rmsnorm(x, g):
    ms = jnp.mean(x * x, axis=-1, keepdims=True)
    return x * lax.rsqrt(ms + EPS) * g


def _resident(shape):
    zeros = (0,) * len(shape)
    return pl.BlockSpec(shape, lambda *_: zeros, pipeline_mode=pl.Buffered(1))


def _params(n_grid_axes):
    return pltpu.CompilerParams(dimension_semantics=("arbitrary",) * n_grid_axes,
                                vmem_limit_bytes=VMEM_LIMIT_BYTES)


CONV_TOKENS = 1024
CONV_HALO = 8
CONV_CHUNK = 256
CONV_OUT_GROUP = 4


def _conv_layer_kernel(xp_ref, x_ref, xn_ref, g_ref, w_ref, cw_ref, cb_ref, wo_ref, *rest):
    n_cast = (len(rest) - 2) // 2
    o_ref, shift_ref = rest[n_cast], rest[-1]
    for src_ref, dst_ref in zip(rest[:n_cast], rest[n_cast + 1:-1]):
        dst_ref[...] = src_ref[...].astype(BF16)
    i = pl.program_id(1)
    last = pl.num_programs(1) - 1
    tm, halo, ce = CONV_TOKENS, CONV_HALO, CONV_CHUNK
    e = wo_ref.shape[0]
    g = g_ref[...]
    x = x_ref[0]
    xp = jnp.where(i > 0, xp_ref[0], 0.0)
    xn = jnp.where(i < last, xn_ref[0], 0.0)
    hn32 = _rmsnorm(x, g)
    hn = hn32.astype(BF16)
    hne = jnp.concatenate([_rmsnorm(xp, g), hn32, _rmsnorm(xn, g)], axis=0).astype(BF16)
    o_ref[0] = x

    pending = []
    for j in range(e // ce):
        ch = slice(j * ce, (j + 1) * ce)

        def proj(lhs, part):
            return jnp.dot(lhs, w_ref[:, part * e + j * ce:part * e + (j + 1) * ce],
                           preferred_element_type=F32)

        p = proj(hne, 1) * proj(hne, 2)
        slabs = shift_ref.at[j % 2]
        for c in range(ce // LANES):
            slabs[c] = p[:, c * LANES:(c + 1) * LANES]

        def shifted(by):
            return jnp.concatenate([slabs[c, halo + by:halo + by + tm, :]
                                    for c in range(ce // LANES)], axis=1)

        conv = (cw_ref[0:1, ch] * shifted(-1) + cw_ref[1:2, ch] * p[halo:halo + tm]
                + cw_ref[2:3, ch] * shifted(1) + cb_ref[:, ch])
        y = proj(hn, 0) * conv * _silu(proj(hn, 3))
        pending.append(y.astype(BF16))
        if len(pending) == CONV_OUT_GROUP:
            rows = slice((j + 1 - CONV_OUT_GROUP) * ce, (j + 1) * ce)
            o_ref[0] += jnp.dot(jnp.concatenate(pending, axis=1), wo_ref[rows, :],
                                preferred_element_type=F32)
            pending.clear()


def _conv_layer(x, g, w_in, conv_w, conv_b, w_out, to_bf16=()):
    bsz, s, d = x.shape
    e = w_out.shape[0]
    tm, halo, ce = CONV_TOKENS, CONV_HALO, CONV_CHUNK
    assert s % tm == 0 and tm % halo == 0 and e % ce == 0 and w_in.shape == (d, 4 * e)
    per_tile = tm // halo
    n_halo_blocks = s // halo
    n_steps = bsz * (s // tm)
    assert all(a.shape[0] % (n_steps * BF16_SUBLANES) == 0 for a in to_bf16)
    cast_specs = [pl.BlockSpec((a.shape[0] // n_steps, a.shape[1]),
                               lambda b, i: (b * (s // tm) + i, 0)) for a in to_bf16]
    h, *copies = pl.pallas_call(
        _conv_layer_kernel,
        name="conv_layer",
        grid=(bsz, s // tm),
        in_specs=[
            pl.BlockSpec((1, halo, d), lambda b, i: (b, jnp.maximum(i * per_tile - 1, 0), 0)),
            pl.BlockSpec((1, tm, d), lambda b, i: (b, i, 0)),
            pl.BlockSpec((1, halo, d),
                         lambda b, i: (b, jnp.minimum((i + 1) * per_tile, n_halo_blocks - 1), 0)),
            _resident((1, d)),
            _resident((d, 4 * e)),
            _resident((3, e)),
            _resident((1, e)),
            _resident((e, d)),
            *cast_specs,
        ],
        out_specs=[pl.BlockSpec((1, tm, d), lambda b, i: (b, i, 0)), *cast_specs],
        out_shape=[jax.ShapeDtypeStruct(x.shape, x.dtype),
                   *[jax.ShapeDtypeStruct(a.shape, BF16) for a in to_bf16]],
        scratch_shapes=[pltpu.VMEM((2, ce // LANES, tm + 2 * halo, LANES), F32)],
        compiler_params=_params(2),
    )(x, x, x, g.reshape(1, d), w_in.astype(BF16), conv_w, conv_b.reshape(1, e),
      w_out.astype(BF16), *to_bf16)
    return h, copies


def _attn_proj_kernel(h_ref, g_ref, w_ref, gain_ref, headmean_ref, qkv_ref, z_ref, slab_ref,
                      slab2_ref):
    gw, tm = GROUP_WIDTH, TILE
    n_slabs = slab_ref.shape[0]
    n_qkv = 3 * N_GROUPS
    hn32 = _rmsnorm(h_ref[0], g_ref[...])
    hn = hn32.astype(BF16)
    for c in range(n_slabs):
        slab_ref[c] = hn32[:, c * LANES:(c + 1) * LANES]
    step = UNSPLIT_ROW_STRIDE

    def gather_rows(ref, starts, rows, stride):
        return jnp.concatenate(
            [jnp.concatenate([ref[c, pl.ds(start, rows, stride=stride), :] for start in starts],
                             axis=0) for c in range(n_slabs)], axis=1)

    by_step = gather_rows(slab_ref, range(step), tm // step, step)
    for c in range(N_GROUPS):
        z = jnp.dot(hn, w_ref[:, (n_qkv + c) * gw:(n_qkv + c + 1) * gw],
                    preferred_element_type=F32)
        z_ref[0, :, c * gw:(c + 1) * gw] = _silu(z).astype(BF16)
    for grp, (_, dil) in enumerate(DIL_PAIRS):
        if dil == 1:
            hg = hn
        elif dil == step:
            hg = by_step.astype(BF16)
        else:
            assert dil == step * step
            for c in range(n_slabs):
                slab2_ref[c] = by_step[:, c * LANES:(c + 1) * LANES]
            starts = [(r % step) * (tm // step) + r // step for r in range(dil)]
            hg = gather_rows(slab2_ref, starts, tm // dil, dil // step).astype(BF16)
        for t in range(3):
            c = 3 * grp + t
            y = jnp.dot(hg, w_ref[:, c * gw:(c + 1) * gw], preferred_element_type=F32)
            if t < 2:
                sq = (y * y).astype(BF16)
                ms = jnp.concatenate(
                    [jnp.dot(sq[:, k:k + MXU_DIM], headmean_ref[...], preferred_element_type=F32)
                     for k in range(0, gw, MXU_DIM)], axis=1)
                y = y * lax.rsqrt(ms + EPS) * gain_ref[c]
            qkv_ref[c, 0] = y.astype(BF16)


def _attn_proj(h, g, w_in, q_gain, k_gain):
    bsz, s, d = h.shape
    gw, tm = GROUP_WIDTH, TILE
    n_qkv = 3 * N_GROUPS
    assert s % tm == 0 and d % LANES == 0 and w_in.shape == (d, (n_qkv + N_GROUPS) * gw)
    gains = jnp.stack([q_gain.reshape(N_GROUPS, gw) * (HEAD_DIM ** -0.5 * LOG2_E),
                       k_gain.reshape(N_GROUPS, gw),
                       jnp.ones((N_GROUPS, gw), F32)], axis=1).reshape(n_qkv, 1, gw)
    head = np.arange(MXU_DIM) // HEAD_DIM
    headmean = jnp.asarray((head[:, None] == head[None, :]) / HEAD_DIM, BF16)
    return pl.pallas_call(
        _attn_proj_kernel,
        name="attn_proj",
        grid=(bsz, s // tm),
        in_specs=[
            pl.BlockSpec((1, tm, d), lambda b, i: (b, i, 0)),
            _resident((1, d)),
            _resident(w_in.shape),
            _resident((n_qkv, 1, gw)),
            _resident((MXU_DIM, MXU_DIM)),
        ],
        out_specs=[
            pl.BlockSpec((n_qkv, 1, tm, gw), lambda b, i: (0, b, i, 0)),
            pl.BlockSpec((1, tm, N_GROUPS * gw), lambda b, i: (b, i, 0)),
        ],
        out_shape=[
            jax.ShapeDtypeStruct((n_qkv, bsz, s, gw), BF16),
            jax.ShapeDtypeStruct((bsz, s, N_GROUPS * gw), BF16),
        ],
        scratch_shapes=[pltpu.VMEM((d // LANES, tm, LANES), F32)] * 2,
        compiler_params=_params(2),
    )(h, g.reshape(1, d), w_in.astype(BF16), gains, headmean)


def _class_row_spans(prev_ref, cur_ref, next_ref, start, stop):
    rows = cur_ref.shape[2]
    seg = cur_ref.shape[0] * rows
    spans, pos = [], start
    while pos < stop:
        if pos < 0:
            ref, base, limit = prev_ref, pos + HALF_WINDOW, 0
        elif pos >= seg:
            ref, base, limit = next_ref, pos - seg, stop
        else:
            ref, base, limit = cur_ref, pos, seg
        tile, first = divmod(base, ref.shape[2])
        n = min(min(stop, limit) - pos, ref.shape[2] - first)
        spans.append((ref, tile, first, n))
        pos += n
    return spans


def _band_attn_kernel(q_ref, kp_ref, kc_ref, kn_ref, vp_ref, vc_ref, vn_ref, bias_ref,
                      headmask_ref, o_ref, stat_ref, *, dil):
    hw, qb = HALF_WINDOW, Q_BLOCK
    n_heads = HEADS_PER_GROUP
    blocks_per_class = q_ref.shape[0] * q_ref.shape[2] // qb
    step, last_step = pl.program_id(1), pl.num_programs(1) - 1
    lane = lax.broadcasted_iota(jnp.int32, (qb, LANES), 1)
    low_half = lane < HEAD_DIM

    def gather(spans, r, cols):
        parts = [ref[tile, r, first:first + n, cols] for ref, tile, first, n in spans]
        return parts[0] if len(parts) == 1 else jnp.concatenate(parts, axis=0)

    def scatter(ref, spans, r, cols, value):
        done = 0
        for _, tile, first, n in spans:
            ref[tile, r, first:first + n, cols] = value[done:done + n]
            done += n

    for r in range(dil):
        for u in range(blocks_per_class):
            q_spans = _class_row_spans(None, q_ref, None, u * qb, (u + 1) * qb)
            k_spans = _class_row_spans(kp_ref, kc_ref, kn_ref, u * qb - hw, (u + 1) * qb + hw)
            v_spans = _class_row_spans(vp_ref, vc_ref, vn_ref, u * qb - hw, (u + 1) * qb + hw)
            variant = 1
            if u == 0:
                variant = jnp.where(step == 0, 0, variant)
            if u == blocks_per_class - 1:
                variant = jnp.where(step == last_step, 2, variant)
            stats = jnp.zeros((qb, LANES), F32)
            for pair in range(n_heads // 2):
                cols = slice(pair * LANES, (pair + 1) * LANES)
                q2 = gather(q_spans, r, cols)
                qq = jnp.concatenate([q2 * headmask_ref[0], q2 * headmask_ref[1]], axis=0)
                kw = gather(k_spans, r, cols)
                vw = jnp.concatenate([gather(v_spans, r, cols),
                                      jnp.ones((K_WINDOW, LANES), BF16)], axis=1)
                s = lax.dot_general(qq, kw, (((1,), (1,)), ((), ())),
                                    preferred_element_type=F32)
                s = s + bias_ref[variant, pair]
                m = jnp.max(s, axis=-1, keepdims=True)
                e = jnp.exp2(s - m)
                o2 = jnp.dot(e.astype(BF16), vw, preferred_element_type=F32)
                l = o2[:, LANES:]
                o = jnp.where(low_half, o2[:qb, :LANES], o2[qb:, :LANES]).astype(BF16)
                scatter(o_ref, q_spans, r, cols, o)
                for k in range(2):
                    head = 2 * pair + k
                    stats = jnp.where(lane == head, m[k * qb:(k + 1) * qb], stats)
                    stats = jnp.where(lane == n_heads + head, l[k * qb:(k + 1) * qb], stats)
            scatter(stat_ref, q_spans, r, slice(None), stats)


def _t5_bucket(rel):
    nb = REL_BUCKETS // 2
    ret = (rel > 0).astype(np.int32) * nb
    n = np.abs(rel)
    max_exact = nb // 2
    large = max_exact + (np.log(np.maximum(n, 1) / max_exact)
                         / np.log(REL_MAX_DIST / max_exact) * (nb - max_exact)).astype(np.int32)
    large = np.minimum(large, nb - 1)
    return ret + np.where(n < max_exact, n, large).astype(np.int32)


def _band_bias(rel_table, group, dil):
    hw, qb, kw = HALF_WINDOW, Q_BLOCK, K_WINDOW
    n_heads = HEADS_PER_GROUP
    buckets = _t5_bucket(np.arange(-hw, hw + 1) * dil)
    heads = slice(group * n_heads, (group + 1) * n_heads)
    per_offset = rel_table[buckets][:, heads].astype(F32).T * LOG2_E
    width = qb + kw
    pad = jnp.full((n_heads, qb - 1), MASKED, F32)
    vec = jnp.concatenate([pad, per_offset, pad, jnp.full((n_heads, 2), MASKED, F32)], axis=1)
    assert vec.shape[1] == width + 1
    skew = jnp.tile(vec, (1, qb))[:, :qb * width].reshape(n_heads, qb, width)
    mid = skew[:, :, qb - 1:qb - 1 + kw]
    kj = np.arange(kw)[None, None, :]
    first = jnp.where(kj >= hw, mid, MASKED)
    final = jnp.where(kj < hw + qb, mid, MASKED)
    return jnp.stack([first, mid, final]).reshape(3, n_heads // 2, 2 * qb, kw)


def _band_attention(qkv, group, dil, bias):
    n_qkv, bsz, s, gw = qkv.shape
    hw, qb, tiles = HALF_WINDOW, Q_BLOCK, ATTN_TILES
    rows = TILE // dil
    n_tiles = s // TILE
    n_steps = n_tiles // tiles
    assert s % (TILE * tiles) == 0 and (tiles * rows) % qb == 0 and n_steps >= 2
    assert rows % hw == 0 or hw % rows == 0
    view = qkv.reshape(n_qkv, bsz, n_tiles, dil, rows, gw)
    headmask = jnp.asarray(
        (np.arange(LANES)[None, :] // HEAD_DIM == np.arange(2)[:, None])[:, None, :], BF16)
    if rows >= hw:
        halo_block = (None, None, 1, dil, hw, gw)
        sub = rows // hw
        prev_idx = lambda i: (jnp.maximum(tiles * i - 1, 0), 0, sub - 1, 0)
        next_idx = lambda i: (jnp.minimum(tiles * (i + 1), n_tiles - 1), 0, 0, 0)
    else:
        per = hw // rows
        halo_block = (None, None, per, dil, rows, gw)
        prev_idx = lambda i: (jnp.maximum(tiles // per * i - 1, 0), 0, 0, 0)
        next_idx = lambda i: (jnp.minimum(tiles // per * (i + 1), n_tiles // per - 1), 0, 0, 0)

    def specs(c):
        cur = pl.BlockSpec((None, None, tiles, dil, rows, gw), lambda b, i: (c, b, i, 0, 0, 0))
        prev = pl.BlockSpec(halo_block, lambda b, i: (c, b) + prev_idx(i))
        nxt = pl.BlockSpec(halo_block, lambda b, i: (c, b) + next_idx(i))
        return prev, cur, nxt

    (_, q_spec, _), k_specs, v_specs = specs(3 * group), specs(3 * group + 1), specs(3 * group + 2)
    out_block = lambda width: pl.BlockSpec((None, tiles, dil, rows, width),
                                           lambda b, i: (b, i, 0, 0, 0))
    o, stats = pl.pallas_call(
        functools.partial(_band_attn_kernel, dil=dil),
        name=f"band_attn_d{dil}",
        grid=(bsz, n_steps),
        in_specs=[q_spec, *k_specs, *v_specs, _resident(bias.shape), _resident((2, 1, LANES))],
        out_specs=[out_block(gw), out_block(LANES)],
        out_shape=[jax.ShapeDtypeStruct((bsz, n_tiles, dil, rows, gw), BF16),
                   jax.ShapeDtypeStruct((bsz, n_tiles, dil, rows, LANES), F32)],
        compiler_params=_params(2),
    )(view, view, view, view, view, view, view, bias, headmask)
    return o.reshape(bsz, s, gw), stats.reshape(bsz, s, LANES)


OUT_TOKENS = 2 * TILE


def _attn_out_kernel(h_ref, o0_ref, o1_ref, o2_ref, s0_ref, s1_ref, s2_ref, gate_ref, expand_ref,
                     wo_ref, out_ref, slab_ref):
    gw, tm = GROUP_WIDTH, OUT_TOKENS
    n_heads = HEADS_PER_GROUP

    def restride(x, moves, stride):
        n = x.shape[1] // LANES
        for c in range(n):
            for base in range(0, tm, TILE):
                for src, dst, rows in moves:
                    slab_ref[c, pl.ds(base + dst, rows, stride=stride), :] = (
                        x[base + src:base + src + rows, c * LANES:(c + 1) * LANES])
        return jnp.concatenate([slab_ref[c] for c in range(n)], axis=1)

    def to_token_order(x, dil):
        step = UNSPLIT_ROW_STRIDE
        if dil > step:
            assert dil == step * step
            rows, block = TILE // dil, TILE // step
            x = restride(x, [((r1 + step * r2) * rows, r1 * block + r2, rows)
                             for r1 in range(step) for r2 in range(dil // step)], dil // step)
        if dil > 1:
            rows = TILE // step
            x = restride(x, [(r * rows, r, rows) for r in range(step)], step)
        return x

    dils = [dil for _, dil in DIL_PAIRS]
    ms = [to_token_order(ref[0], dil) for ref, dil in zip((s0_ref, s1_ref, s2_ref), dils)]
    ls = [pltpu.roll(m, LANES - n_heads, axis=1) for m in ms]
    top = jnp.maximum(jnp.maximum(ms[0], ms[1]), ms[2])
    ws = [jnp.exp2(m - top) for m in ms]
    denom = ls[0] * ws[0] + ls[1] * ws[1] + ls[2] * ws[2]
    lane = lax.broadcasted_iota(jnp.int32, (tm, LANES), 1)
    denom = jnp.where(lane < n_heads, denom, 1.0)
    acc = h_ref[0]
    for g, o_ref in enumerate((o0_ref, o1_ref, o2_ref)):
        scale = ws[g] / denom
        hi = scale.astype(BF16)
        lo = (scale - hi.astype(F32)).astype(BF16)
        spread = jnp.dot(jnp.concatenate([hi, lo], axis=1), expand_ref[...],
                         preferred_element_type=F32)
        gate = gate_ref[0, :, g * gw:(g + 1) * gw].astype(F32)
        y = to_token_order(o_ref[0].astype(F32), dils[g]) * spread * gate
        acc = acc + jnp.dot(y.astype(BF16), wo_ref[g], preferred_element_type=F32)
    out_ref[0] = acc


def _attn_out(h, outs, stats, z, w_out):
    bsz, s, d = h.shape
    gw, tm = GROUP_WIDTH, OUT_TOKENS
    assert s % tm == 0 and tm % TILE == 0 and w_out.shape == (N_GROUPS * gw, d)
    expand = jnp.asarray(
        np.arange(2 * LANES)[:, None] % LANES == np.arange(gw)[None, :] // HEAD_DIM, BF16)
    tok = lambda width: pl.BlockSpec((1, tm, width), lambda b, i: (b, i, 0))
    return pl.pallas_call(
        _attn_out_kernel,
        name="attn_out",
        grid=(bsz, s // tm),
        in_specs=[tok(d)] + [tok(gw)] * 3 + [tok(LANES)] * 3 + [
            tok(N_GROUPS * gw), _resident((2 * LANES, gw)), _resident((N_GROUPS, gw, d))],
        out_specs=tok(d),
        out_shape=jax.ShapeDtypeStruct(h.shape, h.dtype),
        scratch_shapes=[pltpu.VMEM((gw // LANES, tm, LANES), F32)],
        compiler_params=_params(2),
    )(h, *outs, *stats, z, expand, w_out.astype(BF16).reshape(N_GROUPS, gw, d))


def _attn_layer(h, g, w_in, q_gain, k_gain, rel_table, w_out):
    qkv, z = _attn_proj(h, g, w_in, q_gain, k_gain)
    outs, stats = [], []
    for grp, (window, dil) in enumerate(DIL_PAIRS):
        assert (window // 2) // dil == HALF_WINDOW
        o, st = _band_attention(qkv, grp, dil, _band_bias(rel_table, grp, dil))
        outs.append(o)
        stats.append(st)
    return _attn_out(h, outs, stats, z, w_out)


def kernel(x, norm_g, conv_w_in, conv_kernel, conv_bias, conv_w_out, attn_w_in, q_norm_g,
           k_norm_g, attn_w_out, rel_bias_table):
    h = x
    depth = norm_g.shape[0]
    for layer in range(depth):
        j = layer // 2
        if layer % 2 == 0:
            nxt = (attn_w_in[j], attn_w_out[j]) if layer + 1 < depth else ()
            h, attn_weights = _conv_layer(h, norm_g[layer], conv_w_in[j], conv_kernel[j],
                                          conv_bias[j], conv_w_out[j], nxt)
        else:
            w_in, w_out = attn_weights
            h = _attn_layer(h, norm_g[layer], w_in, q_norm_g[j], k_norm_g[j], rel_bias_table,
                            w_out)
    return h
```

```python
import functools

import jax
import jax.numpy as jnp
import numpy as np
from jax import lax
from jax.experimental import pallas as pl
from jax.experimental.pallas import tpu as pltpu

EPS = 1e-6
HEAD_DIM = 64
HEADS_PER_GROUP = 8
GROUP_WIDTH = HEAD_DIM * HEADS_PER_GROUP
DIL_PAIRS = ((128, 1), (512, 4), (2048, 16))
N_GROUPS = len(DIL_PAIRS)
HALF_WINDOW = 64
REL_BUCKETS = 32
REL_MAX_DIST = 1024
MASKED = -1e30
LOG2_E = 1.4426950408889634

LANES = 128
BF16_SUBLANES = 16
MXU_DIM = 256
Q_BLOCK = 128
K_WINDOW = Q_BLOCK + 2 * HALF_WINDOW
TILE = 512
ATTN_TILES = 4
UNSPLIT_ROW_STRIDE = 4
VMEM_LIMIT_BYTES = 56 * 1024 * 1024

BF16 = jnp.bfloat16
F32 = jnp.float32


def _silu(z):
    half = 0.5 * z
    return half + half * jnp.tanh(half)


def _rmsnorm(x, g):
    ms = jnp.mean(x * x, axis=-1, keepdims=True)
    return x * lax.rsqrt(ms + EPS) * g


def _resident(shape):
    zeros = (0,) * len(shape)
    return pl.BlockSpec(shape, lambda *_: zeros, pipeline_mode=pl.Buffered(1))


def _params(n_grid_axes):
    return pltpu.CompilerParams(dimension_semantics=("arbitrary",) * n_grid_axes,
                                vmem_limit_bytes=VMEM_LIMIT_BYTES)


CONV_TOKENS = 1024
CONV_HALO = 8
CONV_CHUNK = 256


def _conv_layer_kernel(xp_ref, x_ref, xn_ref, g_ref, w_ref, cw_ref, cb_ref, wo_ref, *rest):
    n_cast = (len(rest) - 2) // 2
    o_ref, shift_ref = rest[n_cast], rest[-1]
    for src_ref, dst_ref in zip(rest[:n_cast], rest[n_cast + 1:-1]):
        dst_ref[...] = src_ref[...].astype(BF16)
    i = pl.program_id(1)
    last = pl.num_programs(1) - 1
    tm, halo, ce = CONV_TOKENS, CONV_HALO, CONV_CHUNK
    e = wo_ref.shape[0]
    g = g_ref[...]
    x = x_ref[0]
    xp = jnp.where(i > 0, xp_ref[0], 0.0)
    xn = jnp.where(i < last, xn_ref[0], 0.0)
    hn32 = _rmsnorm(x, g)
    hn = hn32.astype(BF16)
    hne = jnp.concatenate([_rmsnorm(xp, g), hn32, _rmsnorm(xn, g)], axis=0).astype(BF16)
    o_ref[0] = x

    for j in range(e // ce):
        ch = slice(j * ce, (j + 1) * ce)

        def proj(lhs, part):
            return jnp.dot(lhs, w_ref[:, part * e + j * ce:part * e + (j + 1) * ce],
                           preferred_element_type=F32)

        p = proj(hne, 1) * proj(hne, 2)
        slabs = shift_ref.at[j % 2]
        for c in range(ce // LANES):
            slabs[c] = p[:, c * LANES:(c + 1) * LANES]

        def shifted(by):
            return jnp.concatenate([slabs[c, halo + by:halo + by + tm, :]
                                    for c in range(ce // LANES)], axis=1)

        conv = (cw_ref[0:1, ch] * shifted(-1) + cw_ref[1:2, ch] * p[halo:halo + tm]
                + cw_ref[2:3, ch] * shifted(1) + cb_ref[:, ch])
        y = proj(hn, 0) * conv * _silu(proj(hn, 3))
        o_ref[0] += jnp.dot(y.astype(BF16), wo_ref[ch, :], preferred_element_type=F32)


def _conv_layer(x, g, w_in, conv_w, conv_b, w_out, to_bf16=()):
    bsz, s, d = x.shape
    e = w_out.shape[0]
    tm, halo, ce = CONV_TOKENS, CONV_HALO, CONV_CHUNK
    assert s % tm == 0 and tm % halo == 0 and e % ce == 0 and w_in.shape == (d, 4 * e)
    per_tile = tm // halo
    n_halo_blocks = s // halo
    n_steps = bsz * (s // tm)
    assert all(a.shape[0] % (n_steps * BF16_SUBLANES) == 0 for a in to_bf16)
    cast_specs = [pl.BlockSpec((a.shape[0] // n_steps, a.shape[1]),
                               lambda b, i: (b * (s // tm) + i, 0)) for a in to_bf16]
    h, *copies = pl.pallas_call(
        _conv_layer_kernel,
        name="conv_layer",
        grid=(bsz, s // tm),
        in_specs=[
            pl.BlockSpec((1, halo, d), lambda b, i: (b, jnp.maximum(i * per_tile - 1, 0), 0)),
            pl.BlockSpec((1, tm, d), lambda b, i: (b, i, 0)),
            pl.BlockSpec((1, halo, d),
                         lambda b, i: (b, jnp.minimum((i + 1) * per_tile, n_halo_blocks - 1), 0)),
            _resident((1, d)),
            _resident((d, 4 * e)),
            _resident((3, e)),
            _resident((1, e)),
            _resident((e, d)),
            *cast_specs,
        ],
        out_specs=[pl.BlockSpec((1, tm, d), lambda b, i: (b, i, 0)), *cast_specs],
        out_shape=[jax.ShapeDtypeStruct(x.shape, x.dtype),
                   *[jax.ShapeDtypeStruct(a.shape, BF16) for a in to_bf16]],
        scratch_shapes=[pltpu.VMEM((2, ce // LANES, tm + 2 * halo, LANES), F32)],
        compiler_params=_params(2),
    )(x, x, x, g.reshape(1, d), w_in.astype(BF16), conv_w, conv_b.reshape(1, e),
      w_out.astype(BF16), *to_bf16)
    return h, copies


PROJ_TOKENS = 2 * TILE


def _attn_proj_kernel(h_ref, g_ref, w_ref, gain_ref, headmean_ref, qkv_ref, z_ref, slab_ref):
    gw, tm = GROUP_WIDTH, PROJ_TOKENS
    n_slabs = slab_ref.shape[0]
    n_qkv = 3 * N_GROUPS
    hn32 = _rmsnorm(h_ref[0], g_ref[...])
    hn = hn32.astype(BF16)
    for c in range(n_slabs):
        slab_ref[c] = hn32[:, c * LANES:(c + 1) * LANES]
    step = UNSPLIT_ROW_STRIDE

    def gather_rows(ref, starts, rows, stride):
        return jnp.concatenate(
            [jnp.concatenate([ref[c, pl.ds(base + start, rows, stride=stride), :]
                              for base in range(0, tm, TILE) for start in starts], axis=0)
             for c in range(n_slabs)], axis=1)

    by_step = gather_rows(slab_ref, range(step), TILE // step, step)
    for c in range(N_GROUPS):
        z = jnp.dot(hn, w_ref[:, (n_qkv + c) * gw:(n_qkv + c + 1) * gw],
                    preferred_element_type=F32)
        z_ref[0, :, c * gw:(c + 1) * gw] = _silu(z).astype(BF16)
    for grp, (_, dil) in enumerate(DIL_PAIRS):
        if dil == 1:
            hg = hn
        elif dil == step:
            hg = by_step.astype(BF16)
        else:
            assert dil == step * step
            for c in range(n_slabs):
                slab_ref[c] = by_step[:, c * LANES:(c + 1) * LANES]
            starts = [(r % step) * (TILE // step) + r // step for r in range(dil)]
            hg = gather_rows(slab_ref, starts, TILE // dil, dil // step).astype(BF16)
        for t in range(3):
            c = 3 * grp + t
            y = jnp.dot(hg, w_ref[:, c * gw:(c + 1) * gw], preferred_element_type=F32)
            if t < 2:
                sq = (y * y).astype(BF16)
                ms = jnp.concatenate(
                    [jnp.dot(sq[:, k:k + MXU_DIM], headmean_ref[...], preferred_element_type=F32)
                     for k in range(0, gw, MXU_DIM)], axis=1)
                y = y * lax.rsqrt(ms + EPS) * gain_ref[c]
            qkv_ref[c, 0] = y.astype(BF16)


def _attn_proj(h, g, w_in, q_gain, k_gain):
    bsz, s, d = h.shape
    gw, tm = GROUP_WIDTH, PROJ_TOKENS
    n_qkv = 3 * N_GROUPS
    assert s % tm == 0 and tm % TILE == 0 and d % LANES == 0
    assert w_in.shape == (d, (n_qkv + N_GROUPS) * gw)
    gains = jnp.stack([q_gain.reshape(N_GROUPS, gw) * (HEAD_DIM ** -0.5 * LOG2_E),
                       k_gain.reshape(N_GROUPS, gw),
                       jnp.ones((N_GROUPS, gw), F32)], axis=1).reshape(n_qkv, 1, gw)
    head = np.arange(MXU_DIM) // HEAD_DIM
    headmean = jnp.asarray((head[:, None] == head[None, :]) / HEAD_DIM, BF16)
    return pl.pallas_call(
        _attn_proj_kernel,
        name="attn_proj",
        grid=(bsz, s // tm),
        in_specs=[
            pl.BlockSpec((1, tm, d), lambda b, i: (b, i, 0)),
            _resident((1, d)),
            _resident(w_in.shape),
            _resident((n_qkv, 1, gw)),
            _resident((MXU_DIM, MXU_DIM)),
        ],
        out_specs=[
            pl.BlockSpec((n_qkv, 1, tm, gw), lambda b, i: (0, b, i, 0)),
            pl.BlockSpec((1, tm, N_GROUPS * gw), lambda b, i: (b, i, 0)),
        ],
        out_shape=[
            jax.ShapeDtypeStruct((n_qkv, bsz, s, gw), BF16),
            jax.ShapeDtypeStruct((bsz, s, N_GROUPS * gw), BF16),
        ],
        scratch_shapes=[pltpu.VMEM((d // LANES, tm, LANES), F32)],
        compiler_params=_params(2),
    )(h, g.reshape(1, d), w_in.astype(BF16), gains, headmean)


def _class_row_spans(prev_ref, cur_ref, next_ref, start, stop):
    rows = cur_ref.shape[2]
    seg = cur_ref.shape[0] * rows
    spans, pos = [], start
    while pos < stop:
        if pos < 0:
            ref, base, limit = prev_ref, pos + HALF_WINDOW, 0
        elif pos >= seg:
            ref, base, limit = next_ref, pos - seg, stop
        else:
            ref, base, limit = cur_ref, pos, seg
        tile, first = divmod(base, ref.shape[2])
        n = min(min(stop, limit) - pos, ref.shape[2] - first)
        spans.append((ref, tile, first, n))
        pos += n
    return spans


def _band_attn_kernel(q_ref, kp_ref, kc_ref, kn_ref, vp_ref, vc_ref, vn_ref, bias_ref,
                      headmask_ref, o_ref, stat_ref, *, dil):
    hw, qb = HALF_WINDOW, Q_BLOCK
    n_heads = HEADS_PER_GROUP
    blocks_per_class = q_ref.shape[0] * q_ref.shape[2] // qb
    step, last_step = pl.program_id(1), pl.num_programs(1) - 1
    lane = lax.broadcasted_iota(jnp.int32, (qb, LANES), 1)
    low_half = lane < HEAD_DIM

    def gather(spans, r, cols):
        parts = [ref[tile, r, first:first + n, cols] for ref, tile, first, n in spans]
        return parts[0] if len(parts) == 1 else jnp.concatenate(parts, axis=0)

    def scatter(ref, spans, r, cols, value):
        done = 0
        for _, tile, first, n in spans:
            ref[tile, r, first:first + n, cols] = value[done:done + n]
            done += n

    for r in range(dil):
        for u in range(blocks_per_class):
            q_spans = _class_row_spans(None, q_ref, None, u * qb, (u + 1) * qb)
            k_spans = _class_row_spans(kp_ref, kc_ref, kn_ref, u * qb - hw, (u + 1) * qb + hw)
            v_spans = _class_row_spans(vp_ref, vc_ref, vn_ref, u * qb - hw, (u + 1) * qb + hw)
            variant = 1
            if u == 0:
                variant = jnp.where(step == 0, 0, variant)
            if u == blocks_per_class - 1:
                variant = jnp.where(step == last_step, 2, variant)
            stats = jnp.zeros((qb, LANES), F32)
            for pair in range(n_heads // 2):
                cols = slice(pair * LANES, (pair + 1) * LANES)
                q2 = gather(q_spans, r, cols)
                qq = jnp.concatenate([q2 * headmask_ref[0], q2 * headmask_ref[1]], axis=0)
                kw = gather(k_spans, r, cols)
                vw = jnp.concatenate([gather(v_spans, r, cols),
                                      jnp.ones((K_WINDOW, LANES), BF16)], axis=1)
                s = lax.dot_general(qq, kw, (((1,), (1,)), ((), ())),
                                    preferred_element_type=F32)
                s = s + bias_ref[variant, pair]
                m = jnp.max(s, axis=-1, keepdims=True)
                e = jnp.exp2(s - m)
                o2 = jnp.dot(e.astype(BF16), vw, preferred_element_type=F32)
                l = o2[:, LANES:]
                o = jnp.where(low_half, o2[:qb, :LANES], o2[qb:, :LANES]).astype(BF16)
                scatter(o_ref, q_spans, r, cols, o)
                for k in range(2):
                    head = 2 * pair + k
                    stats = jnp.where(lane == head, m[k * qb:(k + 1) * qb], stats)
                    stats = jnp.where(lane == n_heads + head, l[k * qb:(k + 1) * qb], stats)
            scatter(stat_ref, q_spans, r, slice(None), stats)


def _t5_bucket(rel):
    nb = REL_BUCKETS // 2
    ret = (rel > 0).astype(np.int32) * nb
    n = np.abs(rel)
    max_exact = nb // 2
    large = max_exact + (np.log(np.maximum(n, 1) / max_exact)
                         / np.log(REL_MAX_DIST / max_exact) * (nb - max_exact)).astype(np.int32)
    large = np.minimum(large, nb - 1)
    return ret + np.where(n < max_exact, n, large).astype(np.int32)


def _band_bias(rel_table, group, dil):
    hw, qb, kw = HALF_WINDOW, Q_BLOCK, K_WINDOW
    n_heads = HEADS_PER_GROUP
    buckets = _t5_bucket(np.arange(-hw, hw + 1) * dil)
    heads = slice(group * n_heads, (group + 1) * n_heads)
    per_offset = rel_table[buckets][:, heads].astype(F32).T * LOG2_E
    width = qb + kw
    pad = jnp.full((n_heads, qb - 1), MASKED, F32)
    vec = jnp.concatenate([pad, per_offset, pad, jnp.full((n_heads, 2), MASKED, F32)], axis=1)
    assert vec.shape[1] == width + 1
    skew = jnp.tile(vec, (1, qb))[:, :qb * width].reshape(n_heads, qb, width)
    mid = skew[:, :, qb - 1:qb - 1 + kw]
    kj = np.arange(kw)[None, None, :]
    first = jnp.where(kj >= hw, mid, MASKED)
    final = jnp.where(kj < hw + qb, mid, MASKED)
    return jnp.stack([first, mid, final]).reshape(3, n_heads // 2, 2 * qb, kw)


def _band_attention(qkv, group, dil, bias):
    n_qkv, bsz, s, gw = qkv.shape
    hw, qb, tiles = HALF_WINDOW, Q_BLOCK, ATTN_TILES
    rows = TILE // dil
    n_tiles = s // TILE
    n_steps = n_tiles // tiles
    assert s % (TILE * tiles) == 0 and (tiles * rows) % qb == 0 and n_steps >= 2
    assert rows % hw == 0 or hw % rows == 0
    view = qkv.reshape(n_qkv, bsz, n_tiles, dil, rows, gw)
    headmask = jnp.asarray(
        (np.arange(LANES)[None, :] // HEAD_DIM == np.arange(2)[:, None])[:, None, :], BF16)
    if rows >= hw:
        halo_block = (None, None, 1, dil, hw, gw)
        sub = rows // hw
        prev_idx = lambda i: (jnp.maximum(tiles * i - 1, 0), 0, sub - 1, 0)
        next_idx = lambda i: (jnp.minimum(tiles * (i + 1), n_tiles - 1), 0, 0, 0)
    else:
        per = hw // rows
        halo_block = (None, None, per, dil, rows, gw)
        prev_idx = lambda i: (jnp.maximum(tiles // per * i - 1, 0), 0, 0, 0)
        next_idx = lambda i: (jnp.minimum(tiles // per * (i + 1), n_tiles // per - 1), 0, 0, 0)

    def specs(c):
        cur = pl.BlockSpec((None, None, tiles, dil, rows, gw), lambda b, i: (c, b, i, 0, 0, 0))
        prev = pl.BlockSpec(halo_block, lambda b, i: (c, b) + prev_idx(i))
        nxt = pl.BlockSpec(halo_block, lambda b, i: (c, b) + next_idx(i))
        return prev, cur, nxt

    (_, q_spec, _), k_specs, v_specs = specs(3 * group), specs(3 * group + 1), specs(3 * group + 2)
    out_block = lambda width: pl.BlockSpec((None, tiles, dil, rows, width),
                                           lambda b, i: (b, i, 0, 0, 0))
    o, stats = pl.pallas_call(
        functools.partial(_band_attn_kernel, dil=dil),
        name=f"band_attn_d{dil}",
        grid=(bsz, n_steps),
        in_specs=[q_spec, *k_specs, *v_specs, _resident(bias.shape), _resident((2, 1, LANES))],
        out_specs=[out_block(gw), out_block(LANES)],
        out_shape=[jax.ShapeDtypeStruct((bsz, n_tiles, dil, rows, gw), BF16),
                   jax.ShapeDtypeStruct((bsz, n_tiles, dil, rows, LANES), F32)],
        compiler_params=_params(2),
    )(view, view, view, view, view, view, view, bias, headmask)
    return o.reshape(bsz, s, gw), stats.reshape(bsz, s, LANES)


OUT_TOKENS = 2 * TILE


def _attn_out_kernel(h_ref, o0_ref, o1_ref, o2_ref, s0_ref, s1_ref, s2_ref, gate_ref, expand_ref,
                     wo_ref, out_ref, slab_ref):
    gw, tm = GROUP_WIDTH, OUT_TOKENS
    n_heads = HEADS_PER_GROUP

    def restride(x, moves, stride):
        n = x.shape[1] // LANES
        for c in range(n):
            for base in range(0, tm, TILE):
                for src, dst, rows in moves:
                    slab_ref[c, pl.ds(base + dst, rows, stride=stride), :] = (
                        x[base + src:base + src + rows, c * LANES:(c + 1) * LANES])
        return jnp.concatenate([slab_ref[c] for c in range(n)], axis=1)

    def to_token_order(x, dil):
        step = UNSPLIT_ROW_STRIDE
        if dil > step:
            assert dil == step * step
            rows, block = TILE // dil, TILE // step
            x = restride(x, [((r1 + step * r2) * rows, r1 * block + r2, rows)
                             for r1 in range(step) for r2 in range(dil // step)], dil // step)
        if dil > 1:
            rows = TILE // step
            x = restride(x, [(r * rows, r, rows) for r in range(step)], step)
        return x

    dils = [dil for _, dil in DIL_PAIRS]
    ms = [to_token_order(ref[0], dil) for ref, dil in zip((s0_ref, s1_ref, s2_ref), dils)]
    ls = [pltpu.roll(m, LANES - n_heads, axis=1) for m in ms]
    top = jnp.maximum(jnp.maximum(ms[0], ms[1]), ms[2])
    ws = [jnp.exp2(m - top) for m in ms]
    denom = ls[0] * ws[0] + ls[1] * ws[1] + ls[2] * ws[2]
    lane = lax.broadcasted_iota(jnp.int32, (tm, LANES), 1)
    denom = jnp.where(lane < n_heads, denom, 1.0)
    acc = h_ref[0]
    for g, o_ref in enumerate((o0_ref, o1_ref, o2_ref)):
        scale = ws[g] / denom
        hi = scale.astype(BF16)
        lo = (scale - hi.astype(F32)).astype(BF16)
        spread = jnp.dot(jnp.concatenate([hi, lo], axis=1), expand_ref[...],
                         preferred_element_type=F32)
        gate = gate_ref[0, :, g * gw:(g + 1) * gw].astype(F32)
        y = to_token_order(o_ref[0].astype(F32), dils[g]) * spread * gate
        acc = acc + jnp.dot(y.astype(BF16), wo_ref[g], preferred_element_type=F32)
    out_ref[0] = acc


def _attn_out(h, outs, stats, z, w_out):
    bsz, s, d = h.shape
    gw, tm = GROUP_WIDTH, OUT_TOKENS
    assert s % tm == 0 and tm % TILE == 0 and w_out.shape == (N_GROUPS * gw, d)
    expand = jnp.asarray(
        np.arange(2 * LANES)[:, None] % LANES == np.arange(gw)[None, :] // HEAD_DIM, BF16)
    tok = lambda width: pl.BlockSpec((1, tm, width), lambda b, i: (b, i, 0))
    return pl.pallas_call(
        _attn_out_kernel,
        name="attn_out",
        grid=(bsz, s // tm),
        in_specs=[tok(d)] + [tok(gw)] * 3 + [tok(LANES)] * 3 + [
            tok(N_GROUPS * gw), _resident((2 * LANES, gw)), _resident((N_GROUPS, gw, d))],
        out_specs=tok(d),
        out_shape=jax.ShapeDtypeStruct(h.shape, h.dtype),
        scratch_shapes=[pltpu.VMEM((gw // LANES, tm, LANES), F32)],
        compiler_params=_params(2),
    )(h, *outs, *stats, z, expand, w_out.astype(BF16).reshape(N_GROUPS, gw, d))


def _attn_layer(h, g, w_in, q_gain, k_gain, rel_table, w_out):
    qkv, z = _attn_proj(h, g, w_in, q_gain, k_gain)
    outs, stats = [], []
    for grp, (window, dil) in enumerate(DIL_PAIRS):
        assert (window // 2) // dil == HALF_WINDOW
        o, st = _band_attention(qkv, grp, dil, _band_bias(rel_table, grp, dil))
        outs.append(o)
        stats.append(st)
    return _attn_out(h, outs, stats, z, w_out)


def kernel(x, norm_g, conv_w_in, conv_kernel, conv_bias, conv_w_out, attn_w_in, q_norm_g,
           k_norm_g, attn_w_out, rel_bias_table):
    h = x
    depth = norm_g.shape[0]
    for layer in range(depth):
        j = layer // 2
        if layer % 2 == 0:
            nxt = (attn_w_in[j], attn_w_out[j]) if layer + 1 < depth else ()
            h, attn_weights = _conv_layer(h, norm_g[layer], conv_w_in[j], conv_kernel[j],
                                          conv_bias[j], conv_w_out[j], nxt)
        else:
            w_in, w_out = attn_weights
            h = _attn_layer(h, norm_g[layer], w_in, q_norm_g[j], k_norm_g[j], rel_bias_table,
                            w_out)
    return h
```

```python
import functools

import jax
import jax.numpy as jnp
import numpy as np
from jax import lax
from jax.experimental import pallas as pl
from jax.experimental.pallas import tpu as pltpu

EPS = 1e-6
HEAD_DIM = 64
HEADS_PER_GROUP = 8
GROUP_WIDTH = HEAD_DIM * HEADS_PER_GROUP
DIL_PAIRS = ((128, 1), (512, 4), (2048, 16))
N_GROUPS = len(DIL_PAIRS)
HALF_WINDOW = 64
REL_BUCKETS = 32
REL_MAX_DIST = 1024
MASKED = -1e30
LOG2_E = 1.4426950408889634

LANES = 128
BF16_SUBLANES = 16
MXU_DIM = 256
Q_BLOCK = 128
K_WINDOW = Q_BLOCK + 2 * HALF_WINDOW
TILE = 512
ATTN_TILES = 4
UNSPLIT_ROW_STRIDE = 4
VMEM_LIMIT_BYTES = 56 * 1024 * 1024

BF16 = jnp.bfloat16
F32 = jnp.float32


def _silu(z):
    half = 0.5 * z
    return half + half * jnp.tanh(half)


def _rmsnorm(x, g):
    ms = jnp.mean(x * x, axis=-1, keepdims=True)
    return x * lax.rsqrt(ms + EPS) * g


def _resident(shape):
    zeros = (0,) * len(shape)
    return pl.BlockSpec(shape, lambda *_: zeros, pipeline_mode=pl.Buffered(1))


def _params(n_grid_axes):
    return pltpu.CompilerParams(dimension_semantics=("arbitrary",) * n_grid_axes,
                                vmem_limit_bytes=VMEM_LIMIT_BYTES)


CONV_TOKENS = 1024
CONV_HALO = 8
CONV_CHUNK = 256


def _conv_layer_kernel(xp_ref, x_ref, xn_ref, g_ref, w_ref, cw_ref, cb_ref, wo_ref, *rest):
    n_cast = (len(rest) - 2) // 2
    o_ref, shift_ref = rest[n_cast], rest[-1]
    for src_ref, dst_ref in zip(rest[:n_cast], rest[n_cast + 1:-1]):
        dst_ref[...] = src_ref[...].astype(BF16)
    i = pl.program_id(1)
    last = pl.num_programs(1) - 1
    tm, halo, ce = CONV_TOKENS, CONV_HALO, CONV_CHUNK
    e = wo_ref.shape[0]
    g = g_ref[...]
    x = x_ref[0]
    xp = jnp.where(i > 0, xp_ref[0], 0.0)
    xn = jnp.where(i < last, xn_ref[0], 0.0)
    hn32 = _rmsnorm(x, g)
    hn = hn32.astype(BF16)
    hne = jnp.concatenate([_rmsnorm(xp, g), hn32, _rmsnorm(xn, g)], axis=0).astype(BF16)
    o_ref[0] = x

    for j in range(e // ce):
        ch = slice(j * ce, (j + 1) * ce)

        def proj(lhs, part):
            return jnp.dot(lhs, w_ref[:, part * e + j * ce:part * e + (j + 1) * ce],
                           preferred_element_type=F32)

        p = proj(hne, 1) * proj(hne, 2)
        slabs = shift_ref.at[j % 2]
        for c in range(ce // LANES):
            slabs[c] = p[:, c * LANES:(c + 1) * LANES]

        def shifted(by):
            return jnp.concatenate([slabs[c, halo + by:halo + by + tm, :]
                                    for c in range(ce // LANES)], axis=1)

        conv = (cw_ref[0:1, ch] * shifted(-1) + cw_ref[1:2, ch] * p[halo:halo + tm]
                + cw_ref[2:3, ch] * shifted(1) + cb_ref[:, ch])
        y = proj(hn, 0) * conv * _silu(proj(hn, 3))
        o_ref[0] += jnp.dot(y.astype(BF16), wo_ref[ch, :], preferred_element_type=F32)


def _conv_layer(x, g, w_in, conv_w, conv_b, w_out, to_bf16=()):
    bsz, s, d = x.shape
    e = w_out.shape[0]
    tm, halo, ce = CONV_TOKENS, CONV_HALO, CONV_CHUNK
    assert s % tm == 0 and tm % halo == 0 and e % ce == 0 and w_in.shape == (d, 4 * e)
    per_tile = tm // halo
    n_halo_blocks = s // halo
    n_steps = bsz * (s // tm)
    assert all(a.shape[0] % (n_steps * BF16_SUBLANES) == 0 for a in to_bf16)
    cast_specs = [pl.BlockSpec((a.shape[0] // n_steps, a.shape[1]),
                               lambda b, i: (b * (s // tm) + i, 0)) for a in to_bf16]
    h, *copies = pl.pallas_call(
        _conv_layer_kernel,
        name="conv_layer",
        grid=(bsz, s // tm),
        in_specs=[
            pl.BlockSpec((1, halo, d), lambda b, i: (b, jnp.maximum(i * per_tile - 1, 0), 0)),
            pl.BlockSpec((1, tm, d), lambda b, i: (b, i, 0)),
            pl.BlockSpec((1, halo, d),
                         lambda b, i: (b, jnp.minimum((i + 1) * per_tile, n_halo_blocks - 1), 0)),
            _resident((1, d)),
            _resident((d, 4 * e)),
            _resident((3, e)),
            _resident((1, e)),
            _resident((e, d)),
            *cast_specs,
        ],
        out_specs=[pl.BlockSpec((1, tm, d), lambda b, i: (b, i, 0)), *cast_specs],
        out_shape=[jax.ShapeDtypeStruct(x.shape, x.dtype),
                   *[jax.ShapeDtypeStruct(a.shape, BF16) for a in to_bf16]],
        scratch_shapes=[pltpu.VMEM((2, ce // LANES, tm + 2 * halo, LANES), F32)],
        compiler_params=_params(2),
    )(x, x, x, g.reshape(1, d), w_in.astype(BF16), conv_w, conv_b.reshape(1, e),
      w_out.astype(BF16), *to_bf16)
    return h, copies


def _attn_proj_kernel(h_ref, g_ref, w_ref, gain_ref, headmean_ref, qkv_ref, z_ref, slab_ref,
                      slab2_ref):
    gw, tm = GROUP_WIDTH, TILE
    n_slabs = slab_ref.shape[0]
    n_qkv = 3 * N_GROUPS
    hn32 = _rmsnorm(h_ref[0], g_ref[...])
    hn = hn32.astype(BF16)
    for c in range(n_slabs):
        slab_ref[c] = hn32[:, c * LANES:(c + 1) * LANES]
    step = UNSPLIT_ROW_STRIDE

    def gather_rows(ref, starts, rows, stride):
        return jnp.concatenate(
            [jnp.concatenate([ref[c, pl.ds(start, rows, stride=stride), :] for start in starts],
                             axis=0) for c in range(n_slabs)], axis=1)

    by_step = gather_rows(slab_ref, range(step), tm // step, step)
    for c in range(N_GROUPS):
        z = jnp.dot(hn, w_ref[:, (n_qkv + c) * gw:(n_qkv + c + 1) * gw],
                    preferred_element_type=F32)
        z_ref[0, :, c * gw:(c + 1) * gw] = _silu(z).astype(BF16)
    for grp, (_, dil) in enumerate(DIL_PAIRS):
        if dil == 1:
            hg = hn
        elif dil == step:
            hg = by_step.astype(BF16)
        else:
            assert dil == step * step
            for c in range(n_slabs):
                slab2_ref[c] = by_step[:, c * LANES:(c + 1) * LANES]
            starts = [(r % step) * (tm // step) + r // step for r in range(dil)]
            hg = gather_rows(slab2_ref, starts, tm // dil, dil // step).astype(BF16)
        for t in range(3):
            c = 3 * grp + t
            y = jnp.dot(hg, w_ref[:, c * gw:(c + 1) * gw], preferred_element_type=F32)
            if t < 2:
                sq = (y * y).astype(BF16)
                ms = jnp.concatenate(
                    [jnp.dot(sq[:, k:k + MXU_DIM], headmean_ref[...], preferred_element_type=F32)
                     for k in range(0, gw, MXU_DIM)], axis=1)
                y = y * lax.rsqrt(ms + EPS) * gain_ref[c]
            qkv_ref[c, 0] = y.astype(BF16)


def _attn_proj(h, g, w_in, q_gain, k_gain):
    bsz, s, d = h.shape
    gw, tm = GROUP_WIDTH, TILE
    n_qkv = 3 * N_GROUPS
    assert s % tm == 0 and d % LANES == 0 and w_in.shape == (d, (n_qkv + N_GROUPS) * gw)
    gains = jnp.stack([q_gain.reshape(N_GROUPS, gw) * (HEAD_DIM ** -0.5 * LOG2_E),
                       k_gain.reshape(N_GROUPS, gw),
                       jnp.ones((N_GROUPS, gw), F32)], axis=1).reshape(n_qkv, 1, gw)
    head = np.arange(MXU_DIM) // HEAD_DIM
    headmean = jnp.asarray((head[:, None] == head[None, :]) / HEAD_DIM, BF16)
    return pl.pallas_call(
        _attn_proj_kernel,
        name="attn_proj",
        grid=(bsz, s // tm),
        in_specs=[
            pl.BlockSpec((1, tm, d), lambda b, i: (b, i, 0)),
            _resident((1, d)),
            _resident(w_in.shape),
            _resident((n_qkv, 1, gw)),
            _resident((MXU_DIM, MXU_DIM)),
        ],
        out_specs=[
            pl.BlockSpec((n_qkv, 1, tm, gw), lambda b, i: (0, b, i, 0)),
            pl.BlockSpec((1, tm, N_GROUPS * gw), lambda b, i: (b, i, 0)),
        ],
        out_shape=[
            jax.ShapeDtypeStruct((n_qkv, bsz, s, gw), BF16),
            jax.ShapeDtypeStruct((bsz, s, N_GROUPS * gw), BF16),
        ],
        scratch_shapes=[pltpu.VMEM((d // LANES, tm, LANES), F32)] * 2,
        compiler_params=_params(2),
    )(h, g.reshape(1, d), w_in.astype(BF16), gains, headmean)


def _class_row_spans(prev_ref, cur_ref, next_ref, start, stop):
    rows = cur_ref.shape[2]
    seg = cur_ref.shape[0] * rows
    spans, pos = [], start
    while pos < stop:
        if pos < 0:
            ref, base, limit = prev_ref, pos + HALF_WINDOW, 0
        elif pos >= seg:
            ref, base, limit = next_ref, pos - seg, stop
        else:
            ref, base, limit = cur_ref, pos, seg
        tile, first = divmod(base, ref.shape[2])
        n = min(min(stop, limit) - pos, ref.shape[2] - first)
        spans.append((ref, tile, first, n))
        pos += n
    return spans


def _band_attn_kernel(q_ref, kp_ref, kc_ref, kn_ref, vp_ref, vc_ref, vn_ref, bias_ref,
                      headmask_ref, o_ref, stat_ref, *, dil):
    hw, qb = HALF_WINDOW, Q_BLOCK
    n_heads = HEADS_PER_GROUP
    blocks_per_class = q_ref.shape[0] * q_ref.shape[2] // qb
    step, last_step = pl.program_id(1), pl.num_programs(1) - 1
    lane = lax.broadcasted_iota(jnp.int32, (qb, LANES), 1)
    low_half = lane < HEAD_DIM
    key_col = lax.broadcasted_iota(jnp.int32, (1, K_WINDOW), 1)

    def gather(spans, r, cols):
        parts = [ref[tile, r, first:first + n, cols] for ref, tile, first, n in spans]
        return parts[0] if len(parts) == 1 else jnp.concatenate(parts, axis=0)

    def scatter(ref, spans, r, cols, value):
        done = 0
        for _, tile, first, n in spans:
            ref[tile, r, first:first + n, cols] = value[done:done + n]
            done += n

    for r in range(dil):
        for u in range(blocks_per_class):
            q_spans = _class_row_spans(None, q_ref, None, u * qb, (u + 1) * qb)
            k_spans = _class_row_spans(kp_ref, kc_ref, kn_ref, u * qb - hw, (u + 1) * qb + hw)
            v_spans = _class_row_spans(vp_ref, vc_ref, vn_ref, u * qb - hw, (u + 1) * qb + hw)
            edge = None
            if u == 0:
                edge = jnp.where(step == 0, jnp.where(key_col < hw, MASKED, 0.0), 0.0)
            if u == blocks_per_class - 1:
                tail = jnp.where(step == last_step, jnp.where(key_col >= hw + qb, MASKED, 0.0), 0.0)
                edge = tail if edge is None else edge + tail
            stats = jnp.zeros((qb, LANES), F32)
            for pair in range(n_heads // 2):
                cols = slice(pair * LANES, (pair + 1) * LANES)
                q2 = gather(q_spans, r, cols)
                qq = jnp.concatenate([q2 * headmask_ref[0], q2 * headmask_ref[1]], axis=0)
                kw = gather(k_spans, r, cols)
                vw = jnp.concatenate([gather(v_spans, r, cols),
                                      jnp.ones((K_WINDOW, LANES), BF16)], axis=1)
                s = lax.dot_general(qq, kw, (((1,), (1,)), ((), ())),
                                    preferred_element_type=F32)
                s = s + bias_ref[pair]
                if edge is not None:
                    s = s + edge
                m = jnp.max(s, axis=-1, keepdims=True)
                e = jnp.exp2(s - m)
                o2 = jnp.dot(e.astype(BF16), vw, preferred_element_type=F32)
                l = o2[:, LANES:]
                o = jnp.where(low_half, o2[:qb, :LANES], o2[qb:, :LANES]).astype(BF16)
                scatter(o_ref, q_spans, r, cols, o)
                for k in range(2):
                    head = 2 * pair + k
                    stats = jnp.where(lane == head, m[k * qb:(k + 1) * qb], stats)
                    stats = jnp.where(lane == n_heads + head, l[k * qb:(k + 1) * qb], stats)
            scatter(stat_ref, q_spans, r, slice(None), stats)


def _t5_bucket(rel):
    nb = REL_BUCKETS // 2
    ret = (rel > 0).astype(np.int32) * nb
    n = np.abs(rel)
    max_exact = nb // 2
    large = max_exact + (np.log(np.maximum(n, 1) / max_exact)
                         / np.log(REL_MAX_DIST / max_exact) * (nb - max_exact)).astype(np.int32)
    large = np.minimum(large, nb - 1)
    return ret + np.where(n < max_exact, n, large).astype(np.int32)


def _band_bias(rel_table, group, dil):
    hw, qb, kw = HALF_WINDOW, Q_BLOCK, K_WINDOW
    n_heads = HEADS_PER_GROUP
    buckets = _t5_bucket(np.arange(-hw, hw + 1) * dil)
    heads = slice(group * n_heads, (group + 1) * n_heads)
    per_offset = rel_table[buckets][:, heads].astype(F32).T * LOG2_E
    width = qb + kw
    pad = jnp.full((n_heads, qb - 1), MASKED, F32)
    vec = jnp.concatenate([pad, per_offset, pad, jnp.full((n_heads, 2), MASKED, F32)], axis=1)
    assert vec.shape[1] == width + 1
    skew = jnp.tile(vec, (1, qb))[:, :qb * width].reshape(n_heads, qb, width)
    return skew[:, :, qb - 1:qb - 1 + kw].reshape(n_heads // 2, 2 * qb, kw)


def _band_attention(qkv, group, dil, bias):
    n_qkv, bsz, s, gw = qkv.shape
    hw, qb, tiles = HALF_WINDOW, Q_BLOCK, ATTN_TILES
    rows = TILE // dil
    n_tiles = s // TILE
    n_steps = n_tiles // tiles
    assert s % (TILE * tiles) == 0 and (tiles * rows) % qb == 0 and n_steps >= 2
    assert rows % hw == 0 or hw % rows == 0
    view = qkv.reshape(n_qkv, bsz, n_tiles, dil, rows, gw)
    headmask = jnp.asarray(
        (np.arange(LANES)[None, :] // HEAD_DIM == np.arange(2)[:, None])[:, None, :], BF16)
    if rows >= hw:
        halo_block = (None, None, 1, dil, hw, gw)
        sub = rows // hw
        prev_idx = lambda i: (jnp.maximum(tiles * i - 1, 0), 0, sub - 1, 0)
        next_idx = lambda i: (jnp.minimum(tiles * (i + 1), n_tiles - 1), 0, 0, 0)
    else:
        per = hw // rows
        halo_block = (None, None, per, dil, rows, gw)
        prev_idx = lambda i: (jnp.maximum(tiles // per * i - 1, 0), 0, 0, 0)
        next_idx = lambda i: (jnp.minimum(tiles // per * (i + 1), n_tiles // per - 1), 0, 0, 0)

    def specs(c):
        cur = pl.BlockSpec((None, None, tiles, dil, rows, gw), lambda b, i: (c, b, i, 0, 0, 0))
        prev = pl.BlockSpec(halo_block, lambda b, i: (c, b) + prev_idx(i))
        nxt = pl.BlockSpec(halo_block, lambda b, i: (c, b) + next_idx(i))
        return prev, cur, nxt

    (_, q_spec, _), k_specs, v_specs = specs(3 * group), specs(3 * group + 1), specs(3 * group + 2)
    out_block = lambda width: pl.BlockSpec((None, tiles, dil, rows, width),
                                           lambda b, i: (b, i, 0, 0, 0))
    o, stats = pl.pallas_call(
        functools.partial(_band_attn_kernel, dil=dil),
        name=f"band_attn_d{dil}",
        grid=(bsz, n_steps),
        in_specs=[q_spec, *k_specs, *v_specs, _resident(bias.shape), _resident((2, 1, LANES))],
        out_specs=[out_block(gw), out_block(LANES)],
        out_shape=[jax.ShapeDtypeStruct((bsz, n_tiles, dil, rows, gw), BF16),
                   jax.ShapeDtypeStruct((bsz, n_tiles, dil, rows, LANES), F32)],
        compiler_params=_params(2),
    )(view, view, view, view, view, view, view, bias, headmask)
    return o.reshape(bsz, s, gw), stats.reshape(bsz, s, LANES)


OUT_TOKENS = 2 * TILE


def _attn_out_kernel(h_ref, o0_ref, o1_ref, o2_ref, s0_ref, s1_ref, s2_ref, gate_ref, expand_ref,
                     wo_ref, out_ref, slab_ref):
    gw, tm = GROUP_WIDTH, OUT_TOKENS
    n_heads = HEADS_PER_GROUP

    def restride(x, moves, stride):
        n = x.shape[1] // LANES
        for c in range(n):
            for base in range(0, tm, TILE):
                for src, dst, rows in moves:
                    slab_ref[c, pl.ds(base + dst, rows, stride=stride), :] = (
                        x[base + src:base + src + rows, c * LANES:(c + 1) * LANES])
        return jnp.concatenate([slab_ref[c] for c in range(n)], axis=1)

    def to_token_order(x, dil):
        step = UNSPLIT_ROW_STRIDE
        if dil > step:
            assert dil == step * step
            rows, block = TILE // dil, TILE // step
            x = restride(x, [((r1 + step * r2) * rows, r1 * block + r2, rows)
                             for r1 in range(step) for r2 in range(dil // step)], dil // step)
        if dil > 1:
            rows = TILE // step
            x = restride(x, [(r * rows, r, rows) for r in range(step)], step)
        return x

    dils = [dil for _, dil in DIL_PAIRS]
    ms = [to_token_order(ref[0], dil) for ref, dil in zip((s0_ref, s1_ref, s2_ref), dils)]
    ls = [pltpu.roll(m, LANES - n_heads, axis=1) for m in ms]
    top = jnp.maximum(jnp.maximum(ms[0], ms[1]), ms[2])
    ws = [jnp.exp2(m - top) for m in ms]
    denom = ls[0] * ws[0] + ls[1] * ws[1] + ls[2] * ws[2]
    lane = lax.broadcasted_iota(jnp.int32, (tm, LANES), 1)
    denom = jnp.where(lane < n_heads, denom, 1.0)
    acc = h_ref[0]
    for g, o_ref in enumerate((o0_ref, o1_ref, o2_ref)):
        scale = ws[g] / denom
        hi = scale.astype(BF16)
        lo = (scale - hi.astype(F32)).astype(BF16)
        spread = jnp.dot(jnp.concatenate([hi, lo], axis=1), expand_ref[...],
                         preferred_element_type=F32)
        gate = gate_ref[0, :, g * gw:(g + 1) * gw].astype(F32)
        y = to_token_order(o_ref[0].astype(F32), dils[g]) * spread * gate
        acc = acc + jnp.dot(y.astype(BF16), wo_ref[g], preferred_element_type=F32)
    out_ref[0] = acc


def _attn_out(h, outs, stats, z, w_out):
    bsz, s, d = h.shape
    gw, tm = GROUP_WIDTH, OUT_TOKENS
    assert s % tm == 0 and tm % TILE == 0 and w_out.shape == (N_GROUPS * gw, d)
    expand = jnp.asarray(
        np.arange(2 * LANES)[:, None] % LANES == np.arange(gw)[None, :] // HEAD_DIM, BF16)
    tok = lambda width: pl.BlockSpec((1, tm, width), lambda b, i: (b, i, 0))
    return pl.pallas_call(
        _attn_out_kernel,
        name="attn_out",
        grid=(bsz, s // tm),
        in_specs=[tok(d)] + [tok(gw)] * 3 + [tok(LANES)] * 3 + [
            tok(N_GROUPS * gw), _resident((2 * LANES, gw)), _resident((N_GROUPS, gw, d))],
        out_specs=tok(d),
        out_shape=jax.ShapeDtypeStruct(h.shape, h.dtype),
        scratch_shapes=[pltpu.VMEM((gw // LANES, tm, LANES), F32)],
        compiler_params=_params(2),
    )(h, *outs, *stats, z, expand, w_out.astype(BF16).reshape(N_GROUPS, gw, d))


def _attn_layer(h, g, w_in, q_gain, k_gain, rel_table, w_out):
    qkv, z = _attn_proj(h, g, w_in, q_gain, k_gain)
    outs, stats = [], []
    for grp, (window, dil) in enumerate(DIL_PAIRS):
        assert (window // 2) // dil == HALF_WINDOW
        o, st = _band_attention(qkv, grp, dil, _band_bias(rel_table, grp, dil))
        outs.append(o)
        stats.append(st)
    return _attn_out(h, outs, stats, z, w_out)


def kernel(x, norm_g, conv_w_in, conv_kernel, conv_bias, conv_w_out, attn_w_in, q_norm_g,
           k_norm_g, attn_w_out, rel_bias_table):
    h = x
    depth = norm_g.shape[0]
    for layer in range(depth):
        j = layer // 2
        if layer % 2 == 0:
            nxt = (attn_w_in[j], attn_w_out[j]) if layer + 1 < depth else ()
            h, attn_weights = _conv_layer(h, norm_g[layer], conv_w_in[j], conv_kernel[j],
                                          conv_bias[j], conv_w_out[j], nxt)
        else:
            w_in, w_out = attn_weights
            h = _attn_layer(h, norm_g[layer], w_in, q_norm_g[j], k_norm_g[j], rel_bias_table,
                            w_out)
    return h
```

```python
import functools

import jax
import jax.numpy as jnp
import numpy as np
from jax import lax
from jax.experimental import pallas as pl
from jax.experimental.pallas import tpu as pltpu

EPS = 1e-6
HEAD_DIM = 64
HEADS_PER_GROUP = 8
GROUP_WIDTH = HEAD_DIM * HEADS_PER_GROUP
DIL_PAIRS = ((128, 1), (512, 4), (2048, 16))
N_GROUPS = len(DIL_PAIRS)
HALF_WINDOW = 64
REL_BUCKETS = 32
REL_MAX_DIST = 1024
MASKED = -1e30
LOG2_E = 1.4426950408889634

LANES = 128
BF16_SUBLANES = 16
MXU_DIM = 256
Q_BLOCK = 128
K_WINDOW = Q_BLOCK + 2 * HALF_WINDOW
TILE = 512
ATTN_TILES = 4
UNSPLIT_ROW_STRIDE = 4
VMEM_LIMIT_BYTES = 56 * 1024 * 1024

BF16 = jnp.bfloat16
F32 = jnp.float32


def _silu(z):
    half = 0.5 * z
    return half + half * jnp.tanh(half)


def _rmsnorm(x, g):
    ms = jnp.mean(x * x, axis=-1, keepdims=True)
    return x * lax.rsqrt(ms + EPS) * g


def _resident(shape):
    zeros = (0,) * len(shape)
    return pl.BlockSpec(shape, lambda *_: zeros, pipeline_mode=pl.Buffered(1))


def _params(n_grid_axes):
    return pltpu.CompilerParams(dimension_semantics=("arbitrary",) * n_grid_axes,
                                vmem_limit_bytes=VMEM_LIMIT_BYTES)


CONV_TOKENS = 1024
CONV_HALO = 8
CONV_CHUNK = 256


CONV_N_SCRATCH = 6
CONV_STAGE = 256


def _conv_layer_kernel(xp_ref, x_ref, xn_ref, g_ref, w_hbm, cw_ref, cb_ref, wo_hbm, *rest):
    n_cast = (len(rest) - 1 - CONV_N_SCRATCH) // 2
    o_ref = rest[n_cast]
    shift_ref, w_ref, wo_ref, stage_in, stage_out, sems = rest[-CONV_N_SCRATCH:]
    for src_ref, dst_ref in zip(rest[:n_cast], rest[n_cast + 1:-CONV_N_SCRATCH]):
        dst_ref[...] = src_ref[...].astype(BF16)
    i = pl.program_id(1)
    last = pl.num_programs(1) - 1
    tm, halo, ce = CONV_TOKENS, CONV_HALO, CONV_CHUNK
    e = wo_ref.shape[0]

    @pl.when(jnp.logical_and(pl.program_id(0) == 0, i == 0))
    def _load_weights():
        st = CONV_STAGE
        jobs = [(w_hbm.at[:, pl.ds(k * st, st)], stage_in, w_ref.at[:, pl.ds(k * st, st)])
                for k in range(w_ref.shape[1] // st)]
        jobs += [(wo_hbm.at[pl.ds(k * st, st), :], stage_out, wo_ref.at[pl.ds(k * st, st), :])
                 for k in range(e // st)]

        def copy(n):
            src, stage, _ = jobs[n]
            return pltpu.make_async_copy(src, stage.at[n % 2], sems.at[n % 2])

        copy(0).start()
        for n, (_, stage, dst) in enumerate(jobs):
            if n + 1 < len(jobs):
                copy(n + 1).start()
            copy(n).wait()
            dst[...] = stage[n % 2].astype(BF16)

    g = g_ref[...]
    x = x_ref[0]
    xp = jnp.where(i > 0, xp_ref[0], 0.0)
    xn = jnp.where(i < last, xn_ref[0], 0.0)
    hn32 = _rmsnorm(x, g)
    hn = hn32.astype(BF16)
    hne = jnp.concatenate([_rmsnorm(xp, g), hn32, _rmsnorm(xn, g)], axis=0).astype(BF16)
    o_ref[0] = x

    for j in range(e // ce):
        ch = slice(j * ce, (j + 1) * ce)

        def proj(lhs, part):
            return jnp.dot(lhs, w_ref[:, part * e + j * ce:part * e + (j + 1) * ce],
                           preferred_element_type=F32)

        p = proj(hne, 1) * proj(hne, 2)
        slabs = shift_ref.at[j % 2]
        for c in range(ce // LANES):
            slabs[c] = p[:, c * LANES:(c + 1) * LANES]

        def shifted(by):
            return jnp.concatenate([slabs[c, halo + by:halo + by + tm, :]
                                    for c in range(ce // LANES)], axis=1)

        conv = (cw_ref[0:1, ch] * shifted(-1) + cw_ref[1:2, ch] * p[halo:halo + tm]
                + cw_ref[2:3, ch] * shifted(1) + cb_ref[:, ch])
        y = proj(hn, 0) * conv * _silu(proj(hn, 3))
        o_ref[0] += jnp.dot(y.astype(BF16), wo_ref[ch, :], preferred_element_type=F32)


def _conv_layer(x, g, w_in, conv_w, conv_b, w_out, to_bf16=()):
    bsz, s, d = x.shape
    e = w_out.shape[0]
    tm, halo, ce = CONV_TOKENS, CONV_HALO, CONV_CHUNK
    assert s % tm == 0 and tm % halo == 0 and e % ce == 0 and w_in.shape == (d, 4 * e)
    assert e % CONV_STAGE == 0 and w_in.dtype == F32 and w_out.dtype == F32
    per_tile = tm // halo
    n_halo_blocks = s // halo
    n_steps = bsz * (s // tm)
    assert all(a.shape[0] % (n_steps * BF16_SUBLANES) == 0 for a in to_bf16)
    cast_specs = [pl.BlockSpec((a.shape[0] // n_steps, a.shape[1]),
                               lambda b, i: (b * (s // tm) + i, 0)) for a in to_bf16]
    h, *copies = pl.pallas_call(
        _conv_layer_kernel,
        name="conv_layer",
        grid=(bsz, s // tm),
        in_specs=[
            pl.BlockSpec((1, halo, d), lambda b, i: (b, jnp.maximum(i * per_tile - 1, 0), 0)),
            pl.BlockSpec((1, tm, d), lambda b, i: (b, i, 0)),
            pl.BlockSpec((1, halo, d),
                         lambda b, i: (b, jnp.minimum((i + 1) * per_tile, n_halo_blocks - 1), 0)),
            _resident((1, d)),
            pl.BlockSpec(memory_space=pl.ANY),
            _resident((3, e)),
            _resident((1, e)),
            pl.BlockSpec(memory_space=pl.ANY),
            *cast_specs,
        ],
        out_specs=[pl.BlockSpec((1, tm, d), lambda b, i: (b, i, 0)), *cast_specs],
        out_shape=[jax.ShapeDtypeStruct(x.shape, x.dtype),
                   *[jax.ShapeDtypeStruct(a.shape, BF16) for a in to_bf16]],
        scratch_shapes=[pltpu.VMEM((2, ce // LANES, tm + 2 * halo, LANES), F32),
                        pltpu.VMEM((d, 4 * e), BF16),
                        pltpu.VMEM((e, d), BF16),
                        pltpu.VMEM((2, d, CONV_STAGE), F32),
                        pltpu.VMEM((2, CONV_STAGE, d), F32),
                        pltpu.SemaphoreType.DMA((2,))],
        compiler_params=_params(2),
    )(x, x, x, g.reshape(1, d), w_in, conv_w, conv_b.reshape(1, e), w_out, *to_bf16)
    return h, copies


def _attn_proj_kernel(h_ref, g_ref, w_ref, gain_ref, headmean_ref, qkv_ref, z_ref, slab_ref,
                      slab2_ref):
    gw, tm = GROUP_WIDTH, TILE
    n_slabs = slab_ref.shape[0]
    n_qkv = 3 * N_GROUPS
    hn32 = _rmsnorm(h_ref[0], g_ref[...])
    hn = hn32.astype(BF16)
    for c in range(n_slabs):
        slab_ref[c] = hn32[:, c * LANES:(c + 1) * LANES]
    step = UNSPLIT_ROW_STRIDE

    def gather_rows(ref, starts, rows, stride):
        return jnp.concatenate(
            [jnp.concatenate([ref[c, pl.ds(start, rows, stride=stride), :] for start in starts],
                             axis=0) for c in range(n_slabs)], axis=1)

    by_step = gather_rows(slab_ref, range(step), tm // step, step)
    for c in range(N_GROUPS):
        z = jnp.dot(hn, w_ref[:, (n_qkv + c) * gw:(n_qkv + c + 1) * gw],
                    preferred_element_type=F32)
        z_ref[0, :, c * gw:(c + 1) * gw] = _silu(z).astype(BF16)
    for grp, (_, dil) in enumerate(DIL_PAIRS):
        if dil == 1:
            hg = hn
        elif dil == step:
            hg = by_step.astype(BF16)
        else:
            assert dil == step * step
            for c in range(n_slabs):
                slab2_ref[c] = by_step[:, c * LANES:(c + 1) * LANES]
            starts = [(r % step) * (tm // step) + r // step for r in range(dil)]
            hg = gather_rows(slab2_ref, starts, tm // dil, dil // step).astype(BF16)
        for t in range(3):
            c = 3 * grp + t
            y = jnp.dot(hg, w_ref[:, c * gw:(c + 1) * gw], preferred_element_type=F32)
            if t < 2:
                sq = (y * y).astype(BF16)
                ms = jnp.concatenate(
                    [jnp.dot(sq[:, k:k + MXU_DIM], headmean_ref[...], preferred_element_type=F32)
                     for k in range(0, gw, MXU_DIM)], axis=1)
                y = y * lax.rsqrt(ms + EPS) * gain_ref[c]
            qkv_ref[c, 0] = y.astype(BF16)


def _attn_proj(h, g, w_in, q_gain, k_gain):
    bsz, s, d = h.shape
    gw, tm = GROUP_WIDTH, TILE
    n_qkv = 3 * N_GROUPS
    assert s % tm == 0 and d % LANES == 0 and w_in.shape == (d, (n_qkv + N_GROUPS) * gw)
    gains = jnp.stack([q_gain.reshape(N_GROUPS, gw) * (HEAD_DIM ** -0.5 * LOG2_E),
                       k_gain.reshape(N_GROUPS, gw),
                       jnp.ones((N_GROUPS, gw), F32)], axis=1).reshape(n_qkv, 1, gw)
    head = np.arange(MXU_DIM) // HEAD_DIM
    headmean = jnp.asarray((head[:, None] == head[None, :]) / HEAD_DIM, BF16)
    return pl.pallas_call(
        _attn_proj_kernel,
        name="attn_proj",
        grid=(bsz, s // tm),
        in_specs=[
            pl.BlockSpec((1, tm, d), lambda b, i: (b, i, 0)),
            _resident((1, d)),
            _resident(w_in.shape),
            _resident((n_qkv, 1, gw)),
            _resident((MXU_DIM, MXU_DIM)),
        ],
        out_specs=[
            pl.BlockSpec((n_qkv, 1, tm, gw), lambda b, i: (0, b, i, 0)),
            pl.BlockSpec((1, tm, N_GROUPS * gw), lambda b, i: (b, i, 0)),
        ],
        out_shape=[
            jax.ShapeDtypeStruct((n_qkv, bsz, s, gw), BF16),
            jax.ShapeDtypeStruct((bsz, s, N_GROUPS * gw), BF16),
        ],
        scratch_shapes=[pltpu.VMEM((d // LANES, tm, LANES), F32)] * 2,
        compiler_params=_params(2),
    )(h, g.reshape(1, d), w_in.astype(BF16), gains, headmean)


def _class_row_spans(prev_ref, cur_ref, next_ref, start, stop):
    rows = cur_ref.shape[2]
    seg = cur_ref.shape[0] * rows
    spans, pos = [], start
    while pos < stop:
        if pos < 0:
            ref, base, limit = prev_ref, pos + HALF_WINDOW, 0
        elif pos >= seg:
            ref, base, limit = next_ref, pos - seg, stop
        else:
            ref, base, limit = cur_ref, pos, seg
        tile, first = divmod(base, ref.shape[2])
        n = min(min(stop, limit) - pos, ref.shape[2] - first)
        spans.append((ref, tile, first, n))
        pos += n
    return spans


def _band_attn_kernel(q_ref, kp_ref, kc_ref, kn_ref, vp_ref, vc_ref, vn_ref, bias_ref,
                      headmask_ref, o_ref, stat_ref, *, dil):
    hw, qb = HALF_WINDOW, Q_BLOCK
    n_heads = HEADS_PER_GROUP
    blocks_per_class = q_ref.shape[0] * q_ref.shape[2] // qb
    step, last_step = pl.program_id(1), pl.num_programs(1) - 1
    lane = lax.broadcasted_iota(jnp.int32, (qb, LANES), 1)
    low_half = lane < HEAD_DIM

    def gather(spans, r, cols):
        parts = [ref[tile, r, first:first + n, cols] for ref, tile, first, n in spans]
        return parts[0] if len(parts) == 1 else jnp.concatenate(parts, axis=0)

    def scatter(ref, spans, r, cols, value):
        done = 0
        for _, tile, first, n in spans:
            ref[tile, r, first:first + n, cols] = value[done:done + n]
            done += n

    for r in range(dil):
        for u in range(blocks_per_class):
            q_spans = _class_row_spans(None, q_ref, None, u * qb, (u + 1) * qb)
            k_spans = _class_row_spans(kp_ref, kc_ref, kn_ref, u * qb - hw, (u + 1) * qb + hw)
            v_spans = _class_row_spans(vp_ref, vc_ref, vn_ref, u * qb - hw, (u + 1) * qb + hw)
            variant = 1
            if u == 0:
                variant = jnp.where(step == 0, 0, variant)
            if u == blocks_per_class - 1:
                variant = jnp.where(step == last_step, 2, variant)
            stats = jnp.zeros((qb, LANES), F32)
            for pair in range(n_heads // 2):
                cols = slice(pair * LANES, (pair + 1) * LANES)
                q2 = gather(q_spans, r, cols)
                qq = jnp.concatenate([q2 * headmask_ref[0], q2 * headmask_ref[1]], axis=0)
                kw = gather(k_spans, r, cols)
                vw = jnp.concatenate([gather(v_spans, r, cols),
                                      jnp.ones((K_WINDOW, LANES), BF16)], axis=1)
                s = lax.dot_general(qq, kw, (((1,), (1,)), ((), ())),
                                    preferred_element_type=F32)
                s = s + bias_ref[variant, pair]
                m = jnp.max(s, axis=-1, keepdims=True)
                e = jnp.exp2(s - m)
                o2 = jnp.dot(e.astype(BF16), vw, preferred_element_type=F32)
                l = o2[:, LANES:]
                o = jnp.where(low_half, o2[:qb, :LANES], o2[qb:, :LANES]).astype(BF16)
                scatter(o_ref, q_spans, r, cols, o)
                for k in range(2):
                    head = 2 * pair + k
                    stats = jnp.where(lane == head, m[k * qb:(k + 1) * qb], stats)
                    stats = jnp.where(lane == n_heads + head, l[k * qb:(k + 1) * qb], stats)
            scatter(stat_ref, q_spans, r, slice(None), stats)


def _t5_bucket(rel):
    nb = REL_BUCKETS // 2
    ret = (rel > 0).astype(np.int32) * nb
    n = np.abs(rel)
    max_exact = nb // 2
    large = max_exact + (np.log(np.maximum(n, 1) / max_exact)
                         / np.log(REL_MAX_DIST / max_exact) * (nb - max_exact)).astype(np.int32)
    large = np.minimum(large, nb - 1)
    return ret + np.where(n < max_exact, n, large).astype(np.int32)


def _band_bias(rel_table, group, dil):
    hw, qb, kw = HALF_WINDOW, Q_BLOCK, K_WINDOW
    n_heads = HEADS_PER_GROUP
    buckets = _t5_bucket(np.arange(-hw, hw + 1) * dil)
    heads = slice(group * n_heads, (group + 1) * n_heads)
    per_offset = rel_table[buckets][:, heads].astype(F32).T * LOG2_E
    width = qb + kw
    pad = jnp.full((n_heads, qb - 1), MASKED, F32)
    vec = jnp.concatenate([pad, per_offset, pad, jnp.full((n_heads, 2), MASKED, F32)], axis=1)
    assert vec.shape[1] == width + 1
    skew = jnp.tile(vec, (1, qb))[:, :qb * width].reshape(n_heads, qb, width)
    mid = skew[:, :, qb - 1:qb - 1 + kw]
    kj = np.arange(kw)[None, None, :]
    first = jnp.where(kj >= hw, mid, MASKED)
    final = jnp.where(kj < hw + qb, mid, MASKED)
    return jnp.stack([first, mid, final]).reshape(3, n_heads // 2, 2 * qb, kw)


def _band_attention(qkv, group, dil, bias):
    n_qkv, bsz, s, gw = qkv.shape
    hw, qb, tiles = HALF_WINDOW, Q_BLOCK, ATTN_TILES
    rows = TILE // dil
    n_tiles = s // TILE
    n_steps = n_tiles // tiles
    assert s % (TILE * tiles) == 0 and (tiles * rows) % qb == 0 and n_steps >= 2
    assert rows % hw == 0 or hw % rows == 0
    view = qkv.reshape(n_qkv, bsz, n_tiles, dil, rows, gw)
    headmask = jnp.asarray(
        (np.arange(LANES)[None, :] // HEAD_DIM == np.arange(2)[:, None])[:, None, :], BF16)
    if rows >= hw:
        halo_block = (None, None, 1, dil, hw, gw)
        sub = rows // hw
        prev_idx = lambda i: (jnp.maximum(tiles * i - 1, 0), 0, sub - 1, 0)
        next_idx = lambda i: (jnp.minimum(tiles * (i + 1), n_tiles - 1), 0, 0, 0)
    else:
        per = hw // rows
        halo_block = (None, None, per, dil, rows, gw)
        prev_idx = lambda i: (jnp.maximum(tiles // per * i - 1, 0), 0, 0, 0)
        next_idx = lambda i: (jnp.minimum(tiles // per * (i + 1), n_tiles // per - 1), 0, 0, 0)

    def specs(c):
        cur = pl.BlockSpec((None, None, tiles, dil, rows, gw), lambda b, i: (c, b, i, 0, 0, 0))
        prev = pl.BlockSpec(halo_block, lambda b, i: (c, b) + prev_idx(i))
        nxt = pl.BlockSpec(halo_block, lambda b, i: (c, b) + next_idx(i))
        return prev, cur, nxt

    (_, q_spec, _), k_specs, v_specs = specs(3 * group), specs(3 * group + 1), specs(3 * group + 2)
    out_block = lambda width: pl.BlockSpec((None, tiles, dil, rows, width),
                                           lambda b, i: (b, i, 0, 0, 0))
    o, stats = pl.pallas_call(
        functools.partial(_band_attn_kernel, dil=dil),
        name=f"band_attn_d{dil}",
        grid=(bsz, n_steps),
        in_specs=[q_spec, *k_specs, *v_specs, _resident(bias.shape), _resident((2, 1, LANES))],
        out_specs=[out_block(gw), out_block(LANES)],
        out_shape=[jax.ShapeDtypeStruct((bsz, n_tiles, dil, rows, gw), BF16),
                   jax.ShapeDtypeStruct((bsz, n_tiles, dil, rows, LANES), F32)],
        compiler_params=_params(2),
    )(view, view, view, view, view, view, view, bias, headmask)
    return o.reshape(bsz, s, gw), stats.reshape(bsz, s, LANES)


OUT_TOKENS = 2 * TILE


def _attn_out_kernel(h_ref, o0_ref, o1_ref, o2_ref, s0_ref, s1_ref, s2_ref, gate_ref, expand_ref,
                     wo_ref, out_ref, slab_ref):
    gw, tm = GROUP_WIDTH, OUT_TOKENS
    n_heads = HEADS_PER_GROUP

    def restride(x, moves, stride):
        n = x.shape[1] // LANES
        for c in range(n):
            for base in range(0, tm, TILE):
                for src, dst, rows in moves:
                    slab_ref[c, pl.ds(base + dst, rows, stride=stride), :] = (
                        x[base + src:base + src + rows, c * LANES:(c + 1) * LANES])
        return jnp.concatenate([slab_ref[c] for c in range(n)], axis=1)

    def to_token_order(x, dil):
        step = UNSPLIT_ROW_STRIDE
        if dil > step:
            assert dil == step * step
            rows, block = TILE // dil, TILE // step
            x = restride(x, [((r1 + step * r2) * rows, r1 * block + r2, rows)
                             for r1 in range(step) for r2 in range(dil // step)], dil // step)
        if dil > 1:
            rows = TILE // step
            x = restride(x, [(r * rows, r, rows) for r in range(step)], step)
        return x

    dils = [dil for _, dil in DIL_PAIRS]
    ms = [to_token_order(ref[0], dil) for ref, dil in zip((s0_ref, s1_ref, s2_ref), dils)]
    ls = [pltpu.roll(m, LANES - n_heads, axis=1) for m in ms]
    top = jnp.maximum(jnp.maximum(ms[0], ms[1]), ms[2])
    ws = [jnp.exp2(m - top) for m in ms]
    denom = ls[0] * ws[0] + ls[1] * ws[1] + ls[2] * ws[2]
    lane = lax.broadcasted_iota(jnp.int32, (tm, LANES), 1)
    denom = jnp.where(lane < n_heads, denom, 1.0)
    acc = h_ref[0]
    for g, o_ref in enumerate((o0_ref, o1_ref, o2_ref)):
        scale = ws[g] / denom
        hi = scale.astype(BF16)
        lo = (scale - hi.astype(F32)).astype(BF16)
        spread = jnp.dot(jnp.concatenate([hi, lo], axis=1), expand_ref[...],
                         preferred_element_type=F32)
        gate = gate_ref[0, :, g * gw:(g + 1) * gw].astype(F32)
        y = to_token_order(o_ref[0].astype(F32), dils[g]) * spread * gate
        acc = acc + jnp.dot(y.astype(BF16), wo_ref[g], preferred_element_type=F32)
    out_ref[0] = acc


def _attn_out(h, outs, stats, z, w_out):
    bsz, s, d = h.shape
    gw, tm = GROUP_WIDTH, OUT_TOKENS
    assert s % tm == 0 and tm % TILE == 0 and w_out.shape == (N_GROUPS * gw, d)
    expand = jnp.asarray(
        np.arange(2 * LANES)[:, None] % LANES == np.arange(gw)[None, :] // HEAD_DIM, BF16)
    tok = lambda width: pl.BlockSpec((1, tm, width), lambda b, i: (b, i, 0))
    return pl.pallas_call(
        _attn_out_kernel,
        name="attn_out",
        grid=(bsz, s // tm),
        in_specs=[tok(d)] + [tok(gw)] * 3 + [tok(LANES)] * 3 + [
            tok(N_GROUPS * gw), _resident((2 * LANES, gw)), _resident((N_GROUPS, gw, d))],
        out_specs=tok(d),
        out_shape=jax.ShapeDtypeStruct(h.shape, h.dtype),
        scratch_shapes=[pltpu.VMEM((gw // LANES, tm, LANES), F32)],
        compiler_params=_params(2),
    )(h, *outs, *stats, z, expand, w_out.astype(BF16).reshape(N_GROUPS, gw, d))


def _attn_layer(h, g, w_in, q_gain, k_gain, rel_table, w_out):
    qkv, z = _attn_proj(h, g, w_in, q_gain, k_gain)
    outs, stats = [], []
    for grp, (window, dil) in enumerate(DIL_PAIRS):
        assert (window // 2) // dil == HALF_WINDOW
        o, st = _band_attention(qkv, grp, dil, _band_bias(rel_table, grp, dil))
        outs.append(o)
        stats.append(st)
    return _attn_out(h, outs, stats, z, w_out)


def kernel(x, norm_g, conv_w_in, conv_kernel, conv_bias, conv_w_out, attn_w_in, q_norm_g,
           k_norm_g, attn_w_out, rel_bias_table):
    h = x
    depth = norm_g.shape[0]
    for layer in range(depth):
        j = layer // 2
        if layer % 2 == 0:
            nxt = (attn_w_in[j], attn_w_out[j]) if layer + 1 < depth else ()
            h, attn_weights = _conv_layer(h, norm_g[layer], conv_w_in[j], conv_kernel[j],
                                          conv_bias[j], conv_w_out[j], nxt)
        else:
            w_in, w_out = attn_weights
            h = _attn_layer(h, norm_g[layer], w_in, q_norm_g[j], k_norm_g[j], rel_bias_table,
                            w_out)
    return h
```

```python
import functools

import jax
import jax.numpy as jnp
import numpy as np
from jax import lax
from jax.experimental import pallas as pl
from jax.experimental.pallas import tpu as pltpu

EPS = 1e-6
HEAD_DIM = 64
HEADS_PER_GROUP = 8
GROUP_WIDTH = HEAD_DIM * HEADS_PER_GROUP
DIL_PAIRS = ((128, 1), (512, 4), (2048, 16))
N_GROUPS = len(DIL_PAIRS)
HALF_WINDOW = 64
REL_BUCKETS = 32
REL_MAX_DIST = 1024
MASKED = -1e30
LOG2_E = 1.4426950408889634

LANES = 128
BF16_SUBLANES = 16
MXU_DIM = 256
Q_BLOCK = 128
K_WINDOW = Q_BLOCK + 2 * HALF_WINDOW
TILE = 512
ATTN_TILES = 4
UNSPLIT_ROW_STRIDE = 4
VMEM_LIMIT_BYTES = 56 * 1024 * 1024

BF16 = jnp.bfloat16
F32 = jnp.float32


def _silu(z):
    half = 0.5 * z
    return half + half * jnp.tanh(half)


def _rmsnorm(x, g):
    ms = jnp.mean(x * x, axis=-1, keepdims=True)
    return x * lax.rsqrt(ms + EPS) * g


def _resident(shape):
    zeros = (0,) * len(shape)
    return pl.BlockSpec(shape, lambda *_: zeros, pipeline_mode=pl.Buffered(1))


def _params(n_grid_axes):
    return pltpu.CompilerParams(dimension_semantics=("arbitrary",) * n_grid_axes,
                                vmem_limit_bytes=VMEM_LIMIT_BYTES)


CONV_TOKENS = 1024
CONV_HALO = 8
CONV_CHUNK = 256


CONV_N_SCRATCH = 6
CONV_STAGE = 256


def _conv_layer_kernel(xp_ref, x_ref, xn_ref, g_ref, w_hbm, cw_ref, cb_ref, wo_hbm, *rest):
    n_cast = (len(rest) - 1 - CONV_N_SCRATCH) // 2
    o_ref = rest[n_cast]
    shift_ref, w_ref, wo_ref, stage_in, stage_out, sems = rest[-CONV_N_SCRATCH:]
    for src_ref, dst_ref in zip(rest[:n_cast], rest[n_cast + 1:-CONV_N_SCRATCH]):
        dst_ref[...] = src_ref[...].astype(BF16)
    i = pl.program_id(1)
    last = pl.num_programs(1) - 1
    tm, halo, ce = CONV_TOKENS, CONV_HALO, CONV_CHUNK
    e = wo_ref.shape[0]

    @pl.when(jnp.logical_and(pl.program_id(0) == 0, i == 0))
    def _load_weights():
        st = CONV_STAGE
        rows_in = stage_in.shape[1]
        jobs = [(w_hbm.at[pl.ds(k * rows_in, rows_in), :], stage_in,
                 w_ref.at[pl.ds(k * rows_in, rows_in), :])
                for k in range(w_ref.shape[0] // rows_in)]
        jobs += [(wo_hbm.at[pl.ds(k * st, st), :], stage_out, wo_ref.at[pl.ds(k * st, st), :])
                 for k in range(e // st)]

        def copy(n):
            src, stage, _ = jobs[n]
            return pltpu.make_async_copy(src, stage.at[n % 2], sems.at[n % 2])

        copy(0).start()
        for n, (_, stage, dst) in enumerate(jobs):
            if n + 1 < len(jobs):
                copy(n + 1).start()
            copy(n).wait()
            dst[...] = stage[n % 2].astype(BF16)

    g = g_ref[...]
    x = x_ref[0]
    xp = jnp.where(i > 0, xp_ref[0], 0.0)
    xn = jnp.where(i < last, xn_ref[0], 0.0)
    hn32 = _rmsnorm(x, g)
    hn = hn32.astype(BF16)
    hne = jnp.concatenate([_rmsnorm(xp, g), hn32, _rmsnorm(xn, g)], axis=0).astype(BF16)
    o_ref[0] = x

    for j in range(e // ce):
        ch = slice(j * ce, (j + 1) * ce)

        def proj(lhs, part):
            return jnp.dot(lhs, w_ref[:, part * e + j * ce:part * e + (j + 1) * ce],
                           preferred_element_type=F32)

        p = proj(hne, 1) * proj(hne, 2)
        slabs = shift_ref.at[j % 2]
        for c in range(ce // LANES):
            slabs[c] = p[:, c * LANES:(c + 1) * LANES]

        def shifted(by):
            return jnp.concatenate([slabs[c, halo + by:halo + by + tm, :]
                                    for c in range(ce // LANES)], axis=1)

        conv = (cw_ref[0:1, ch] * shifted(-1) + cw_ref[1:2, ch] * p[halo:halo + tm]
                + cw_ref[2:3, ch] * shifted(1) + cb_ref[:, ch])
        y = proj(hn, 0) * conv * _silu(proj(hn, 3))
        o_ref[0] += jnp.dot(y.astype(BF16), wo_ref[ch, :], preferred_element_type=F32)


def _conv_layer(x, g, w_in, conv_w, conv_b, w_out, to_bf16=()):
    bsz, s, d = x.shape
    e = w_out.shape[0]
    tm, halo, ce = CONV_TOKENS, CONV_HALO, CONV_CHUNK
    assert s % tm == 0 and tm % halo == 0 and e % ce == 0 and w_in.shape == (d, 4 * e)
    assert e % CONV_STAGE == 0 and w_in.dtype == F32 and w_out.dtype == F32
    per_tile = tm // halo
    n_halo_blocks = s // halo
    n_steps = bsz * (s // tm)
    assert all(a.shape[0] % (n_steps * BF16_SUBLANES) == 0 for a in to_bf16)
    cast_specs = [pl.BlockSpec((a.shape[0] // n_steps, a.shape[1]),
                               lambda b, i: (b * (s // tm) + i, 0)) for a in to_bf16]
    h, *copies = pl.pallas_call(
        _conv_layer_kernel,
        name="conv_layer",
        grid=(bsz, s // tm),
        in_specs=[
            pl.BlockSpec((1, halo, d), lambda b, i: (b, jnp.maximum(i * per_tile - 1, 0), 0)),
            pl.BlockSpec((1, tm, d), lambda b, i: (b, i, 0)),
            pl.BlockSpec((1, halo, d),
                         lambda b, i: (b, jnp.minimum((i + 1) * per_tile, n_halo_blocks - 1), 0)),
            _resident((1, d)),
            pl.BlockSpec(memory_space=pl.ANY),
            _resident((3, e)),
            _resident((1, e)),
            pl.BlockSpec(memory_space=pl.ANY),
            *cast_specs,
        ],
        out_specs=[pl.BlockSpec((1, tm, d), lambda b, i: (b, i, 0)), *cast_specs],
        out_shape=[jax.ShapeDtypeStruct(x.shape, x.dtype),
                   *[jax.ShapeDtypeStruct(a.shape, BF16) for a in to_bf16]],
        scratch_shapes=[pltpu.VMEM((2, ce // LANES, tm + 2 * halo, LANES), F32),
                        pltpu.VMEM((d, 4 * e), BF16),
                        pltpu.VMEM((e, d), BF16),
                        pltpu.VMEM((2, CONV_STAGE * d // (4 * e), 4 * e), F32),
                        pltpu.VMEM((2, CONV_STAGE, d), F32),
                        pltpu.SemaphoreType.DMA((2,))],
        compiler_params=_params(2),
    )(x, x, x, g.reshape(1, d), w_in, conv_w, conv_b.reshape(1, e), w_out, *to_bf16)
    return h, copies


def _attn_proj_kernel(h_ref, g_ref, w_ref, gain_ref, headmean_ref, qkv_ref, z_ref, slab_ref,
                      slab2_ref):
    gw, tm = GROUP_WIDTH, TILE
    n_slabs = slab_ref.shape[0]
    n_qkv = 3 * N_GROUPS
    hn32 = _rmsnorm(h_ref[0], g_ref[...])
    hn = hn32.astype(BF16)
    for c in range(n_slabs):
        slab_ref[c] = hn32[:, c * LANES:(c + 1) * LANES]
    step = UNSPLIT_ROW_STRIDE

    def gather_rows(ref, starts, rows, stride):
        return jnp.concatenate(
            [jnp.concatenate([ref[c, pl.ds(start, rows, stride=stride), :] for start in starts],
                             axis=0) for c in range(n_slabs)], axis=1)

    by_step = gather_rows(slab_ref, range(step), tm // step, step)
    for c in range(N_GROUPS):
        z = jnp.dot(hn, w_ref[:, (n_qkv + c) * gw:(n_qkv + c + 1) * gw],
                    preferred_element_type=F32)
        z_ref[0, :, c * gw:(c + 1) * gw] = _silu(z).astype(BF16)
    for grp, (_, dil) in enumerate(DIL_PAIRS):
        if dil == 1:
            hg = hn
        elif dil == step:
            hg = by_step.astype(BF16)
        else:
            assert dil == step * step
            for c in range(n_slabs):
                slab2_ref[c] = by_step[:, c * LANES:(c + 1) * LANES]
            starts = [(r % step) * (tm // step) + r // step for r in range(dil)]
            hg = gather_rows(slab2_ref, starts, tm // dil, dil // step).astype(BF16)
        for t in range(3):
            c = 3 * grp + t
            y = jnp.dot(hg, w_ref[:, c * gw:(c + 1) * gw], preferred_element_type=F32)
            if t < 2:
                sq = (y * y).astype(BF16)
                ms = jnp.concatenate(
                    [jnp.dot(sq[:, k:k + MXU_DIM], headmean_ref[...], preferred_element_type=F32)
                     for k in range(0, gw, MXU_DIM)], axis=1)
                y = y * lax.rsqrt(ms + EPS) * gain_ref[c]
            qkv_ref[c, 0] = y.astype(BF16)


def _attn_proj(h, g, w_in, q_gain, k_gain):
    bsz, s, d = h.shape
    gw, tm = GROUP_WIDTH, TILE
    n_qkv = 3 * N_GROUPS
    assert s % tm == 0 and d % LANES == 0 and w_in.shape == (d, (n_qkv + N_GROUPS) * gw)
    gains = jnp.stack([q_gain.reshape(N_GROUPS, gw) * (HEAD_DIM ** -0.5 * LOG2_E),
                       k_gain.reshape(N_GROUPS, gw),
                       jnp.ones((N_GROUPS, gw), F32)], axis=1).reshape(n_qkv, 1, gw)
    head = np.arange(MXU_DIM) // HEAD_DIM
    headmean = jnp.asarray((head[:, None] == head[None, :]) / HEAD_DIM, BF16)
    return pl.pallas_call(
        _attn_proj_kernel,
        name="attn_proj",
        grid=(bsz, s // tm),
        in_specs=[
            pl.BlockSpec((1, tm, d), lambda b, i: (b, i, 0)),
            _resident((1, d)),
            _resident(w_in.shape),
            _resident((n_qkv, 1, gw)),
            _resident((MXU_DIM, MXU_DIM)),
        ],
        out_specs=[
            pl.BlockSpec((n_qkv, 1, tm, gw), lambda b, i: (0, b, i, 0)),
            pl.BlockSpec((1, tm, N_GROUPS * gw), lambda b, i: (b, i, 0)),
        ],
        out_shape=[
            jax.ShapeDtypeStruct((n_qkv, bsz, s, gw), BF16),
            jax.ShapeDtypeStruct((bsz, s, N_GROUPS * gw), BF16),
        ],
        scratch_shapes=[pltpu.VMEM((d // LANES, tm, LANES), F32)] * 2,
        compiler_params=_params(2),
    )(h, g.reshape(1, d), w_in.astype(BF16), gains, headmean)


def _class_row_spans(prev_ref, cur_ref, next_ref, start, stop):
    rows = cur_ref.shape[2]
    seg = cur_ref.shape[0] * rows
    spans, pos = [], start
    while pos < stop:
        if pos < 0:
            ref, base, limit = prev_ref, pos + HALF_WINDOW, 0
        elif pos >= seg:
            ref, base, limit = next_ref, pos - seg, stop
        else:
            ref, base, limit = cur_ref, pos, seg
        tile, first = divmod(base, ref.shape[2])
        n = min(min(stop, limit) - pos, ref.shape[2] - first)
        spans.append((ref, tile, first, n))
        pos += n
    return spans


def _band_attn_kernel(q_ref, kp_ref, kc_ref, kn_ref, vp_ref, vc_ref, vn_ref, bias_ref,
                      headmask_ref, o_ref, stat_ref, *, dil):
    hw, qb = HALF_WINDOW, Q_BLOCK
    n_heads = HEADS_PER_GROUP
    blocks_per_class = q_ref.shape[0] * q_ref.shape[2] // qb
    step, last_step = pl.program_id(1), pl.num_programs(1) - 1
    lane = lax.broadcasted_iota(jnp.int32, (qb, LANES), 1)
    low_half = lane < HEAD_DIM

    def gather(spans, r, cols):
        parts = [ref[tile, r, first:first + n, cols] for ref, tile, first, n in spans]
        return parts[0] if len(parts) == 1 else jnp.concatenate(parts, axis=0)

    def scatter(ref, spans, r, cols, value):
        done = 0
        for _, tile, first, n in spans:
            ref[tile, r, first:first + n, cols] = value[done:done + n]
            done += n

    for r in range(dil):
        for u in range(blocks_per_class):
            q_spans = _class_row_spans(None, q_ref, None, u * qb, (u + 1) * qb)
            k_spans = _class_row_spans(kp_ref, kc_ref, kn_ref, u * qb - hw, (u + 1) * qb + hw)
            v_spans = _class_row_spans(vp_ref, vc_ref, vn_ref, u * qb - hw, (u + 1) * qb + hw)
            variant = 1
            if u == 0:
                variant = jnp.where(step == 0, 0, variant)
            if u == blocks_per_class - 1:
                variant = jnp.where(step == last_step, 2, variant)
            stats = jnp.zeros((qb, LANES), F32)
            for pair in range(n_heads // 2):
                cols = slice(pair * LANES, (pair + 1) * LANES)
                q2 = gather(q_spans, r, cols)
                qq = jnp.concatenate([q2 * headmask_ref[0], q2 * headmask_ref[1]], axis=0)
                kw = gather(k_spans, r, cols)
                vw = jnp.concatenate([gather(v_spans, r, cols),
                                      jnp.ones((K_WINDOW, LANES), BF16)], axis=1)
                s = lax.dot_general(qq, kw, (((1,), (1,)), ((), ())),
                                    preferred_element_type=F32)
                s = s + bias_ref[variant, pair]
                m = jnp.max(s, axis=-1, keepdims=True)
                e = jnp.exp2(s - m)
                o2 = jnp.dot(e.astype(BF16), vw, preferred_element_type=F32)
                l = o2[:, LANES:]
                o = jnp.where(low_half, o2[:qb, :LANES], o2[qb:, :LANES]).astype(BF16)
                scatter(o_ref, q_spans, r, cols, o)
                for k in range(2):
                    head = 2 * pair + k
                    stats = jnp.where(lane == head, m[k * qb:(k + 1) * qb], stats)
                    stats = jnp.where(lane == n_heads + head, l[k * qb:(k + 1) * qb], stats)
            scatter(stat_ref, q_spans, r, slice(None), stats)


def _t5_bucket(rel):
    nb = REL_BUCKETS // 2
    ret = (rel > 0).astype(np.int32) * nb
    n = np.abs(rel)
    max_exact = nb // 2
    large = max_exact + (np.log(np.maximum(n, 1) / max_exact)
                         / np.log(REL_MAX_DIST / max_exact) * (nb - max_exact)).astype(np.int32)
    large = np.minimum(large, nb - 1)
    return ret + np.where(n < max_exact, n, large).astype(np.int32)


def _band_bias(rel_table, group, dil):
    hw, qb, kw = HALF_WINDOW, Q_BLOCK, K_WINDOW
    n_heads = HEADS_PER_GROUP
    buckets = _t5_bucket(np.arange(-hw, hw + 1) * dil)
    heads = slice(group * n_heads, (group + 1) * n_heads)
    per_offset = rel_table[buckets][:, heads].astype(F32).T * LOG2_E
    width = qb + kw
    pad = jnp.full((n_heads, qb - 1), MASKED, F32)
    vec = jnp.concatenate([pad, per_offset, pad, jnp.full((n_heads, 2), MASKED, F32)], axis=1)
    assert vec.shape[1] == width + 1
    skew = jnp.tile(vec, (1, qb))[:, :qb * width].reshape(n_heads, qb, width)
    mid = skew[:, :, qb - 1:qb - 1 + kw]
    kj = np.arange(kw)[None, None, :]
    first = jnp.where(kj >= hw, mid, MASKED)
    final = jnp.where(kj < hw + qb, mid, MASKED)
    return jnp.stack([first, mid, final]).reshape(3, n_heads // 2, 2 * qb, kw)


def _band_attention(qkv, group, dil, bias):
    n_qkv, bsz, s, gw = qkv.shape
    hw, qb, tiles = HALF_WINDOW, Q_BLOCK, ATTN_TILES
    rows = TILE // dil
    n_tiles = s // TILE
    n_steps = n_tiles // tiles
    assert s % (TILE * tiles) == 0 and (tiles * rows) % qb == 0 and n_steps >= 2
    assert rows % hw == 0 or hw % rows == 0
    view = qkv.reshape(n_qkv, bsz, n_tiles, dil, rows, gw)
    headmask = jnp.asarray(
        (np.arange(LANES)[None, :] // HEAD_DIM == np.arange(2)[:, None])[:, None, :], BF16)
    if rows >= hw:
        halo_block = (None, None, 1, dil, hw, gw)
        sub = rows // hw
        prev_idx = lambda i: (jnp.maximum(tiles * i - 1, 0), 0, sub - 1, 0)
        next_idx = lambda i: (jnp.minimum(tiles * (i + 1), n_tiles - 1), 0, 0, 0)
    else:
        per = hw // rows
        halo_block = (None, None, per, dil, rows, gw)
        prev_idx = lambda i: (jnp.maximum(tiles // per * i - 1, 0), 0, 0, 0)
        next_idx = lambda i: (jnp.minimum(tiles // per * (i + 1), n_tiles // per - 1), 0, 0, 0)

    def specs(c):
        cur = pl.BlockSpec((None, None, tiles, dil, rows, gw), lambda b, i: (c, b, i, 0, 0, 0))
        prev = pl.BlockSpec(halo_block, lambda b, i: (c, b) + prev_idx(i))
        nxt = pl.BlockSpec(halo_block, lambda b, i: (c, b) + next_idx(i))
        return prev, cur, nxt

    (_, q_spec, _), k_specs, v_specs = specs(3 * group), specs(3 * group + 1), specs(3 * group + 2)
    out_block = lambda width: pl.BlockSpec((None, tiles, dil, rows, width),
                                           lambda b, i: (b, i, 0, 0, 0))
    o, stats = pl.pallas_call(
        functools.partial(_band_attn_kernel, dil=dil),
        name=f"band_attn_d{dil}",
        grid=(bsz, n_steps),
        in_specs=[q_spec, *k_specs, *v_specs, _resident(bias.shape), _resident((2, 1, LANES))],
        out_specs=[out_block(gw), out_block(LANES)],
        out_shape=[jax.ShapeDtypeStruct((bsz, n_tiles, dil, rows, gw), BF16),
                   jax.ShapeDtypeStruct((bsz, n_tiles, dil, rows, LANES), F32)],
        compiler_params=_params(2),
    )(view, view, view, view, view, view, view, bias, headmask)
    return o.reshape(bsz, s, gw), stats.reshape(bsz, s, LANES)


OUT_TOKENS = 2 * TILE


def _attn_out_kernel(h_ref, o0_ref, o1_ref, o2_ref, s0_ref, s1_ref, s2_ref, gate_ref, expand_ref,
                     wo_ref, out_ref, slab_ref):
    gw, tm = GROUP_WIDTH, OUT_TOKENS
    n_heads = HEADS_PER_GROUP

    def restride(x, moves, stride):
        n = x.shape[1] // LANES
        for c in range(n):
            for base in range(0, tm, TILE):
                for src, dst, rows in moves:
                    slab_ref[c, pl.ds(base + dst, rows, stride=stride), :] = (
                        x[base + src:base + src + rows, c * LANES:(c + 1) * LANES])
        return jnp.concatenate([slab_ref[c] for c in range(n)], axis=1)

    def to_token_order(x, dil):
        step = UNSPLIT_ROW_STRIDE
        if dil > step:
            assert dil == step * step
            rows, block = TILE // dil, TILE // step
            x = restride(x, [((r1 + step * r2) * rows, r1 * block + r2, rows)
                             for r1 in range(step) for r2 in range(dil // step)], dil // step)
        if dil > 1:
            rows = TILE // step
            x = restride(x, [(r * rows, r, rows) for r in range(step)], step)
        return x

    dils = [dil for _, dil in DIL_PAIRS]
    ms = [to_token_order(ref[0], dil) for ref, dil in zip((s0_ref, s1_ref, s2_ref), dils)]
    ls = [pltpu.roll(m, LANES - n_heads, axis=1) for m in ms]
    top = jnp.maximum(jnp.maximum(ms[0], ms[1]), ms[2])
    ws = [jnp.exp2(m - top) for m in ms]
    denom = ls[0] * ws[0] + ls[1] * ws[1] + ls[2] * ws[2]
    lane = lax.broadcasted_iota(jnp.int32, (tm, LANES), 1)
    denom = jnp.where(lane < n_heads, denom, 1.0)
    acc = h_ref[0]
    for g, o_ref in enumerate((o0_ref, o1_ref, o2_ref)):
        scale = ws[g] / denom
        hi = scale.astype(BF16)
        lo = (scale - hi.astype(F32)).astype(BF16)
        spread = jnp.dot(jnp.concatenate([hi, lo], axis=1), expand_ref[...],
                         preferred_element_type=F32)
        gate = gate_ref[0, :, g * gw:(g + 1) * gw].astype(F32)
        y = to_token_order(o_ref[0].astype(F32), dils[g]) * spread * gate
        acc = acc + jnp.dot(y.astype(BF16), wo_ref[g], preferred_element_type=F32)
    out_ref[0] = acc


def _attn_out(h, outs, stats, z, w_out):
    bsz, s, d = h.shape
    gw, tm = GROUP_WIDTH, OUT_TOKENS
    assert s % tm == 0 and tm % TILE == 0 and w_out.shape == (N_GROUPS * gw, d)
    expand = jnp.asarray(
        np.arange(2 * LANES)[:, None] % LANES == np.arange(gw)[None, :] // HEAD_DIM, BF16)
    tok = lambda width: pl.BlockSpec((1, tm, width), lambda b, i: (b, i, 0))
    return pl.pallas_call(
        _attn_out_kernel,
        name="attn_out",
        grid=(bsz, s // tm),
        in_specs=[tok(d)] + [tok(gw)] * 3 + [tok(LANES)] * 3 + [
            tok(N_GROUPS * gw), _resident((2 * LANES, gw)), _resident((N_GROUPS, gw, d))],
        out_specs=tok(d),
        out_shape=jax.ShapeDtypeStruct(h.shape, h.dtype),
        scratch_shapes=[pltpu.VMEM((gw // LANES, tm, LANES), F32)],
        compiler_params=_params(2),
    )(h, *outs, *stats, z, expand, w_out.astype(BF16).reshape(N_GROUPS, gw, d))


def _attn_layer(h, g, w_in, q_gain, k_gain, rel_table, w_out):
    qkv, z = _attn_proj(h, g, w_in, q_gain, k_gain)
    outs, stats = [], []
    for grp, (window, dil) in enumerate(DIL_PAIRS):
        assert (window // 2) // dil == HALF_WINDOW
        o, st = _band_attention(qkv, grp, dil, _band_bias(rel_table, grp, dil))
        outs.append(o)
        stats.append(st)
    return _attn_out(h, outs, stats, z, w_out)


def kernel(x, norm_g, conv_w_in, conv_kernel, conv_bias, conv_w_out, attn_w_in, q_norm_g,
           k_norm_g, attn_w_out, rel_bias_table):
    h = x
    depth = norm_g.shape[0]
    for layer in range(depth):
        j = layer // 2
        if layer % 2 == 0:
            nxt = (attn_w_in[j], attn_w_out[j]) if layer + 1 < depth else ()
            h, attn_weights = _conv_layer(h, norm_g[layer], conv_w_in[j], conv_kernel[j],
                                          conv_bias[j], conv_w_out[j], nxt)
        else:
            w_in, w_out = attn_weights
            h = _attn_layer(h, norm_g[layer], w_in, q_norm_g[j], k_norm_g[j], rel_bias_table,
                            w_out)
    return h
```
